```python
import jax
import jax.numpy as jnp
from jax import lax
import numpy as np

D_MODEL = 1024
BATCH = 16
SEQ = 4096
DEPTH = 4

CTX_LEN = 256
GRID_W = 64
HEAD_DIM = 64
D_MIX = D_MODEL
D_LRU = D_MIX // 2
LRU_BLOCKS = D_LRU // HEAD_DIM
LRU_BLOCK = D_LRU // LRU_BLOCKS
LRU_CONV_W = 4
LRU_PAD = (2, 1)
LRU_C = 8.0
GA_HEADS = D_MIX // 4 // HEAD_DIM
GA_KV = GA_HEADS // 2
WA_HEADS = D_MIX // 4 // HEAD_DIM
WA_KV = WA_HEADS // 2
WINDOW = 128
Q_BLOCK = 128
D_FF = 2816
FFN_CONV_W = 3
FFN_PAD = (1, 1)
ROPE_BASE = 10000.0
EPS = 1e-6
NEG_INF = -1e30
ATTN_SCALE = HEAD_DIM ** -0.5
IN_SPLITS = (D_LRU, D_LRU, GA_HEADS * HEAD_DIM, GA_KV * HEAD_DIM, GA_KV * HEAD_DIM,
             WA_HEADS * HEAD_DIM, WA_KV * HEAD_DIM, WA_KV * HEAD_DIM)
D_IN = sum(IN_SPLITS)

kernel_name = "hymba_style_rglru_gqa_swa_convffn_dit"


def rms_norm(t, g):
    tf = t.astype(jnp.float32)
    y = tf * lax.rsqrt(jnp.mean(tf * tf, axis=-1, keepdims=True) + EPS)
    return (y * g.astype(jnp.float32)).astype(t.dtype)


def modulate(t, shift, scale):
    return t * (1.0 + scale) + shift


def dwconv(t, w, b, pad_l, pad_r):
    ch = t.shape[-1]
    y = lax.conv_general_dilated(t, w[:, None, :], window_strides=(1,), padding=[(pad_l, pad_r)],
                                 dimension_numbers=("NWC", "WIO", "NWC"), feature_group_count=ch)
    return y + b


def split_heads(t, n_heads):
    return t.reshape(t.shape[0], t.shape[1], n_heads, HEAD_DIM)


def axial_rope_tables(n):
    rows = n // GRID_W
    row = jnp.repeat(jnp.arange(rows), GRID_W).astype(jnp.float32)
    col = jnp.tile(jnp.arange(GRID_W), rows).astype(jnp.float32)
    n_freq = HEAD_DIM // 4
    inv = ROPE_BASE ** (-jnp.arange(n_freq, dtype=jnp.float32) / n_freq)
    ang = jnp.concatenate([row[:, None] * inv, col[:, None] * inv], axis=-1)
    return jnp.cos(ang), jnp.sin(ang)


def apply_rope(t, cos, sin):
    tf = t.astype(jnp.float32)
    half = HEAD_DIM // 2
    t1, t2 = tf[..., :half], tf[..., half:]
    cs, sn = cos[None, :, None, :], sin[None, :, None, :]
    return jnp.concatenate([t1 * cs - t2 * sn, t1 * sn + t2 * cs], axis=-1).astype(t.dtype)


def _lru_combine(left, right):
    a_l, b_l = left
    a_r, b_r = right
    return a_l * a_r, a_r * b_l + b_r


def rglru_scan(u, w_a, b_a, w_x, b_x, lam, h0, reverse):
    bsz, n, _ = u.shape
    ub = u.reshape(bsz, n, LRU_BLOCKS, LRU_BLOCK)
    r = jax.nn.sigmoid((jnp.einsum("btnc,ncd->btnd", ub, w_a).reshape(bsz, n, D_LRU) + b_a).astype(jnp.float32))
    i = jax.nn.sigmoid((jnp.einsum("btnc,ncd->btnd", ub, w_x).reshape(bsz, n, D_LRU) + b_x).astype(jnp.float32))
    log_a = -LRU_C * r * jax.nn.softplus(-lam.astype(jnp.float32))
    a = jnp.exp(log_a)
    drive = jnp.sqrt(-jnp.expm1(2.0 * log_a)) * i * u.astype(jnp.float32)
    if h0 is not None:
        edge = n - 1 if reverse else 0
        drive = drive.at[:, edge].add(a[:, edge] * h0)
    _, h = lax.associative_scan(_lru_combine, (a, drive), reverse=reverse, axis=1)
    return h


def global_attention(q, k, v):
    bsz, n, n_h, d = q.shape
    g = k.shape[2]
    nb = n // Q_BLOCK
    qb = q.reshape(bsz, nb, Q_BLOCK, g, n_h // g, d).swapaxes(0, 1)

    def one(qn):
        s = jnp.einsum("bqgrd,bkgd->bgrqk", qn, k).astype(jnp.float32)
        p = jax.nn.softmax(s, axis=-1).astype(v.dtype)
        return jnp.einsum("bgrqk,bkgd->bqgrd", p, v)

    o = lax.map(one, qb)
    return o.swapaxes(0, 1).reshape(bsz, n, n_h * d)


def window_attention(q, k, v, k_ctx, v_ctx, sink):
    bsz, n, n_h, d = q.shape
    g = k.shape[2]
    r = n_h // g
    nb = n // Q_BLOCK
    pad = ((0, 0), (Q_BLOCK, Q_BLOCK), (0, 0), (0, 0))

    def band(t):
        tb = jnp.pad(t, pad).reshape(bsz, nb + 2, Q_BLOCK, g, d)
        return jnp.concatenate([tb[:, :-2], tb[:, 1:-1], tb[:, 2:]], axis=2)

    kb, vb = band(k), band(v)
    qb = q.reshape(bsz, nb, Q_BLOCK, g, r, d)
    blk = jnp.arange(nb)
    qpos = blk[:, None] * Q_BLOCK + jnp.arange(Q_BLOCK)[None, :]
    kpos = (blk[:, None] - 1) * Q_BLOCK + jnp.arange(3 * Q_BLOCK)[None, :]
    kp = kpos[:, None, :]
    mask = (jnp.abs(kp - qpos[:, :, None]) <= WINDOW) & (kp >= 0) & (kp < n)
    sink_gr = sink.astype(jnp.float32).reshape(g, r)
    n_loc = 3 * Q_BLOCK
    n_ctx = k_ctx.shape[1]

    def one(args):
        qn, kn, vn, mn = args
        s_loc = jnp.where(mn, jnp.einsum("bqgrd,bkgd->bgrqk", qn, kn).astype(jnp.float32), NEG_INF)
        s_ctx = jnp.einsum("bqgrd,blgd->bgrql", qn, k_ctx).astype(jnp.float32)
        s_sink = jnp.broadcast_to(sink_gr[None, :, :, None, None], s_ctx.shape[:-1] + (1,))
        p = jax.nn.softmax(jnp.concatenate([s_loc, s_ctx, s_sink], axis=-1), axis=-1).astype(vn.dtype)
        return (jnp.einsum("bgrqk,bkgd->bqgrd", p[..., :n_loc], vn)
                + jnp.einsum("bgrql,blgd->bqgrd", p[..., n_loc:n_loc + n_ctx], v_ctx))

    o = lax.map(one, (qb.swapaxes(0, 1), kb.swapaxes(0, 1), vb.swapaxes(0, 1), mask))
    return o.swapaxes(0, 1).reshape(bsz, n, n_h * d)


def sink_attention(q, k, v, sink):
    bsz, n, n_h, d = q.shape
    g = k.shape[2]
    qg = q.reshape(bsz, n, g, n_h // g, d)
    s = jnp.einsum("bqgrd,bkgd->bgrqk", qg, k).astype(jnp.float32)
    s_sink = jnp.broadcast_to(sink.astype(jnp.float32).reshape(g, n_h // g)[None, :, :, None, None], s.shape[:-1] + (1,))
    p = jax.nn.softmax(jnp.concatenate([s, s_sink], axis=-1), axis=-1)[..., :-1].astype(v.dtype)
    return jnp.einsum("bgrqk,bkgd->bqgrd", p, v).reshape(bsz, n, n_h * d)


def hybrid_mixer(a_lat, a_ctx, cos, sin, w_in, conv_w, conv_b, w_a, b_a, w_x, b_x, lam, q_g, k_g, sink, need_ctx):
    offsets = np.cumsum(IN_SPLITS)[:-1].tolist()
    r_l, gate_l, gq_l, gk_l, gv_l, wq_l, wk_l, wv_l = jnp.split(a_lat @ w_in, offsets, axis=-1)
    r_c, gate_c, gq_c, gk_c, gv_c, wq_c, wk_c, wv_c = jnp.split(a_ctx @ w_in, offsets, axis=-1)

    u_l = dwconv(r_l, conv_w, conv_b, LRU_PAD[0], LRU_PAD[1])
    u_c = dwconv(r_c, conv_w, conv_b, LRU_PAD[0], LRU_PAD[1])
    hs_l, hs_c = [], []
    for d_idx, reverse in enumerate((False, True)):
        h_c_d = rglru_scan(u_c, w_a[d_idx], b_a[d_idx], w_x[d_idx], b_x[d_idx], lam[d_idx], None, reverse)
        h0 = h_c_d[:, 0] if reverse else h_c_d[:, -1]
        hs_l.append(rglru_scan(u_l, w_a[d_idx], b_a[d_idx], w_x[d_idx], b_x[d_idx], lam[d_idx], h0, reverse))
        hs_c.append(h_c_d)
    y_a_l = ((hs_l[0] + hs_l[1]) * jax.nn.gelu(gate_l.astype(jnp.float32))).astype(a_lat.dtype)

    kg_c = rms_norm(split_heads(gk_c, GA_KV), k_g)
    vg_c = split_heads(gv_c, GA_KV)
    q_b = apply_rope(rms_norm(split_heads(gq_l, GA_HEADS), q_g), cos, sin) * ATTN_SCALE
    k_b = jnp.concatenate([apply_rope(rms_norm(split_heads(gk_l, GA_KV), k_g), cos, sin), kg_c], axis=1)
    v_b = jnp.concatenate([split_heads(gv_l, GA_KV), vg_c], axis=1)
    y_b_l = global_attention(q_b, k_b, v_b)

    kw_c = split_heads(wk_c, WA_KV)
    vw_c = split_heads(wv_c, WA_KV)
    y_c_l = window_attention(apply_rope(split_heads(wq_l, WA_HEADS), cos, sin) * ATTN_SCALE,
                             apply_rope(split_heads(wk_l, WA_KV), cos, sin), split_heads(wv_l, WA_KV),
                             kw_c, vw_c, sink)

    mix_lat = jnp.concatenate([y_a_l, y_b_l, y_c_l], axis=-1)
    if not need_ctx:
        return mix_lat, None
    y_a_c = ((hs_c[0] + hs_c[1]) * jax.nn.gelu(gate_c.astype(jnp.float32))).astype(a_ctx.dtype)
    y_b_c = global_attention(rms_norm(split_heads(gq_c, GA_HEADS), q_g) * ATTN_SCALE, kg_c, vg_c)
    y_c_c = sink_attention(split_heads(wq_c, WA_HEADS) * ATTN_SCALE, kw_c, vw_c, sink)
    return mix_lat, jnp.concatenate([y_a_c, y_b_c, y_c_c], axis=-1)


def conv_ffn(t, w_up, conv_w, conv_b, w_down):
    u = dwconv(t @ w_up, conv_w, conv_b, FFN_PAD[0], FFN_PAD[1])
    g, v = jnp.split(u, 2, axis=-1)
    return (jax.nn.silu(g) * v) @ w_down


def setup_inputs(seed: int = 0) -> dict:
    key = jax.random.key(seed)
    ks = jax.random.split(key, 26)
    f32 = jnp.float32

    def nrm(k, shape, s):
        return jax.random.normal(k, shape, f32) * s

    u = jax.random.uniform(ks[12], (DEPTH, 2, D_LRU), f32, 0.9, 0.999)
    a0 = u ** (1.0 / LRU_C)
    lam = jnp.log(a0) - jnp.log1p(-a0)
    return {
        "x": nrm(ks[0], (BATCH, SEQ, D_MODEL), 1.0),
        "c": nrm(ks[1], (BATCH, D_MODEL), 1.0),
        "ctx": nrm(ks[2], (BATCH, CTX_LEN, D_MODEL), 1.0),
        "c_ctx": nrm(ks[3], (D_MODEL,), 1.0),
        "w_mod": nrm(ks[4], (DEPTH, D_MODEL, 6 * D_MODEL), 0.5 * D_MODEL ** -0.5),
        "b_mod": nrm(ks[5], (DEPTH, 6 * D_MODEL), 0.02),
        "norm1_g": 1.0 + nrm(ks[6], (DEPTH, D_MODEL), 0.05),
        "w_in": nrm(ks[7], (DEPTH, D_MODEL, D_IN), D_MODEL ** -0.5),
        "lru_conv_w": nrm(ks[8], (DEPTH, LRU_CONV_W, D_LRU), LRU_CONV_W ** -0.5),
        "lru_conv_b": nrm(ks[9], (DEPTH, D_LRU), 0.01),
        "lru_w_a": nrm(ks[10], (DEPTH, 2, LRU_BLOCKS, LRU_BLOCK, LRU_BLOCK), LRU_BLOCK ** -0.5),
        "lru_b_a": nrm(ks[11], (DEPTH, 2, D_LRU), 0.01),
        "lru_w_x": nrm(ks[13], (DEPTH, 2, LRU_BLOCKS, LRU_BLOCK, LRU_BLOCK), LRU_BLOCK ** -0.5),
        "lru_b_x": nrm(ks[14], (DEPTH, 2, D_LRU), 0.01),
        "lru_lam": lam,
        "ga_q_norm_g": 1.0 + nrm(ks[15], (DEPTH, HEAD_DIM), 0.05),
        "ga_k_norm_g": 1.0 + nrm(ks[16], (DEPTH, HEAD_DIM), 0.05),
        "wa_sink": nrm(ks[17], (DEPTH, WA_HEADS), 0.5),
        "w_out": nrm(ks[18], (DEPTH, D_MIX, D_MODEL), D_MIX ** -0.5),
        "norm2_g": 1.0 + nrm(ks[19], (DEPTH, D_MODEL), 0.05),
        "w_up": nrm(ks[20], (DEPTH, D_MODEL, 2 * D_FF), D_MODEL ** -0.5),
        "ffn_conv_w": nrm(ks[21], (DEPTH, FFN_CONV_W, 2 * D_FF), FFN_CONV_W ** -0.5),
        "ffn_conv_b": nrm(ks[22], (DEPTH, 2 * D_FF), 0.01),
        "w_down": nrm(ks[23], (DEPTH, D_FF, D_MODEL), D_FF ** -0.5),
        "final_norm_g": 1.0 + nrm(ks[24], (D_MODEL,), 0.05),
    }


def reference(x, c, ctx, c_ctx, w_mod, b_mod, norm1_g, w_in, lru_conv_w, lru_conv_b, lru_w_a, lru_b_a,
              lru_w_x, lru_b_x, lru_lam, ga_q_norm_g, ga_k_norm_g, wa_sink, w_out, norm2_g, w_up,
              ffn_conv_w, ffn_conv_b, w_down, final_norm_g):
    n = x.shape[1]
    cos, sin = axial_rope_tables(n)
    cond_lat = jax.nn.silu(c)[:, None, :]
    cond_ctx = jax.nn.silu(c_ctx)[None, None, :]
    h, hc = x, ctx
    for l in range(DEPTH):
        need_ctx = l < DEPTH - 1
        sh1, sc1, g1, sh2, sc2, g2 = jnp.split(cond_lat @ w_mod[l] + b_mod[l], 6, axis=-1)
        csh1, csc1, cg1, csh2, csc2, cg2 = jnp.split(cond_ctx @ w_mod[l] + b_mod[l], 6, axis=-1)
        a_lat = modulate(rms_norm(h, norm1_g[l]), sh1, sc1)
        a_ctx = modulate(rms_norm(hc, norm1_g[l]), csh1, csc1)
        mix_lat, mix_ctx = hybrid_mixer(a_lat, a_ctx, cos, sin, w_in[l], lru_conv_w[l], lru_conv_b[l],
                                        lru_w_a[l], lru_b_a[l], lru_w_x[l], lru_b_x[l], lru_lam[l],
                                        ga_q_norm_g[l], ga_k_norm_g[l], wa_sink[l], need_ctx)
        h = h + g1 * (mix_lat @ w_out[l])
        h = h + g2 * conv_ffn(modulate(rms_norm(h, norm2_g[l]), sh2, sc2),
                              w_up[l], ffn_conv_w[l], ffn_conv_b[l], w_down[l])
        if need_ctx:
            hc = hc + cg1 * (mix_ctx @ w_out[l])
            hc = hc + cg2 * conv_ffn(modulate(rms_norm(hc, norm2_g[l]), csh2, csc2),
                                     w_up[l], ffn_conv_w[l], ffn_conv_b[l], w_down[l])
    return rms_norm(h, final_norm_g)
```

```python
import functools

import jax
import jax.numpy as jnp
from jax import lax
from jax.experimental import pallas as pl
from jax.experimental.pallas import tpu as pltpu

F32 = jnp.float32
BF16 = jnp.bfloat16

HEAD_DIM = 64
GRID_W = 64
WINDOW = 128
Q_BLOCK = 128
LRU_C = 8.0
ROPE_BASE = 10000.0
EPS = 1e-6
NEG_INF = -1e30
ATTN_SCALE = HEAD_DIM ** -0.5
LANES = 128
SUBLANES = 8
BF16_ROWS = 16
VMEM_LIMIT = 56 * 1024 * 1024


def _cparams(*sem):
    return pltpu.CompilerParams(dimension_semantics=sem, vmem_limit_bytes=VMEM_LIMIT)


def _rms_rows(x, g):
    ms = jnp.mean(x * x, axis=-1, keepdims=True)
    return x * lax.rsqrt(ms + EPS) * g


def _silu(x):
    return x * jax.nn.sigmoid(x)


def _gelu_tanh(x):
    c = 0.7978845608028654
    return 0.5 * x * (1.0 + jnp.tanh(c * (x + 0.044715 * (x * x * x))))


def _dot(a, b):
    return jnp.dot(a, b, preferred_element_type=F32)


def _dot_nt(a, b):
    return lax.dot_general(a, b, (((1,), (1,)), ((), ())), preferred_element_type=F32)


def _mod_kernel(c_ref, w_ref, b_ref, o_ref):
    s = _silu(c_ref[...])
    o_ref[0] = jnp.dot(s, w_ref[0], preferred_element_type=F32,
                       precision=lax.Precision.HIGHEST) + b_ref[0]


def _modulation(cvec, w_mod, b_mod):
    n_layers, d, d6 = w_mod.shape
    rows = cvec.shape[0]
    tn = 1536
    return pl.pallas_call(
        _mod_kernel,
        out_shape=jax.ShapeDtypeStruct((n_layers, rows, d6), F32),
        grid=(n_layers, d6 // tn),
        in_specs=[pl.BlockSpec((rows, d), lambda l, j: (0, 0)),
                  pl.BlockSpec((1, d, tn), lambda l, j: (l, 0, j)),
                  pl.BlockSpec((1, 1, tn), lambda l, j: (l, 0, j))],
        out_specs=pl.BlockSpec((1, rows, tn), lambda l, j: (l, 0, j)),
        compiler_params=_cparams("parallel", "parallel"),
        name="modulation",
    )(cvec, w_mod, b_mod.reshape(n_layers, 1, d6))


def _seg_mean_sq(x, seg):
    sq = x * x
    hi = sq.astype(BF16)
    lo = (sq - hi.astype(F32)).astype(BF16)
    return (_dot(hi, seg) + _dot(lo, seg)) * (1.0 / HEAD_DIM)


def _swap_halves(x):
    w = x.shape[-1]
    lane = lax.broadcasted_iota(jnp.int32, x.shape, 1)
    first = (lane % HEAD_DIM) < (HEAD_DIM // 2)
    up = pltpu.roll(x, w - HEAD_DIM // 2, axis=1)
    down = pltpu.roll(x, HEAD_DIM // 2, axis=1)
    return jnp.where(first, up, down)


def _dup_kv_heads(k):
    lane = lax.broadcasted_iota(jnp.int32, k.shape, 1)
    lo = lane < HEAD_DIM
    sw = pltpu.roll(k, HEAD_DIM, axis=1)
    return jnp.concatenate([jnp.where(lo, k, sw), jnp.where(lo, sw, k)], axis=1)


def _inproj_kernel(*refs, rope, d_lru):
    if rope:
        (h_ref, mod_ref, g_ref, w_ref, seg_ref, qg_ref, kg_ref, cos_ref, sin_ref,
         r_ref, gate_ref, gq_ref, gk_ref, gv_ref, wq_ref, wk_ref, wv_ref) = refs
    else:
        (h_ref, mod_ref, g_ref, w_ref, seg_ref, qg_ref, kg_ref,
         r_ref, gate_ref, gq_ref, gk_ref, gv_ref, wq_ref, wk_ref, wv_ref) = refs
    x = h_ref[0]
    a = _rms_rows(x, g_ref[...]) * (1.0 + mod_ref[0, 1:2, :]) + mod_ref[0, 0:1, :]
    y = _dot(a.astype(BF16), w_ref[...])
    o = 0
    r_ref[0] = y[:, o:o + d_lru]; o += d_lru
    gate_ref[0] = y[:, o:o + d_lru]; o += d_lru
    gq = y[:, o:o + 256]; o += 256
    gk = y[:, o:o + 128]; o += 128
    gv = y[:, o:o + 128]; o += 128
    wq = y[:, o:o + 256]; o += 256
    wk = y[:, o:o + 128]; o += 128
    wv = y[:, o:o + 128]

    seg = seg_ref[...]
    gq = gq * lax.rsqrt(_seg_mean_sq(gq, seg) + EPS) * qg_ref[...]
    gk = gk * lax.rsqrt(_seg_mean_sq(gk, seg[:128, :128]) + EPS) * kg_ref[...]
    if rope:
        cos = cos_ref[...]
        sin = sin_ref[...]
        gq = gq * cos + _swap_halves(gq) * sin
        wq = wq * cos + _swap_halves(wq) * sin
        gk = gk * cos[:, :128] + _swap_halves(gk) * sin[:, :128]
        wk = wk * cos[:, :128] + _swap_halves(wk) * sin[:, :128]
    gq_ref[0] = (gq * ATTN_SCALE).astype(BF16)
    wq_ref[0] = (wq * ATTN_SCALE).astype(BF16)
    gk_ref[0] = _dup_kv_heads(gk).astype(BF16)
    gv_ref[0] = _dup_kv_heads(gv).astype(BF16)
    wk_ref[0] = _dup_kv_heads(wk).astype(BF16)
    wv_ref[0] = _dup_kv_heads(wv).astype(BF16)


def _in_projection(h, mod, mod_row, g1, w_in, seg, qg, kg, cos, sin, tm):
    bsz, n, d = h.shape
    d_in = w_in.shape[1]
    d_lru = (d_in - 1024) // 2
    rope = cos is not None
    tok = lambda b, i: (b, i, 0)
    const2 = lambda b, i: (0, 0)
    in_specs = [pl.BlockSpec((1, tm, d), tok),
                pl.BlockSpec((1, 6, d), lambda b, i: (mod_row(b), 0, 0)),
                pl.BlockSpec((1, d), const2),
                pl.BlockSpec((d, d_in), const2),
                pl.BlockSpec((256, 256), const2),
                pl.BlockSpec((1, 256), const2),
                pl.BlockSpec((1, 128), const2)]
    args = [h, mod, g1, w_in, seg, qg, kg]
    if rope:
        in_specs += [pl.BlockSpec((tm, 256), lambda b, i: (i, 0)),
                     pl.BlockSpec((tm, 256), lambda b, i: (i, 0))]
        args += [cos, sin]
    out_shape = ([jax.ShapeDtypeStruct((bsz, n, d_lru), F32)] * 2
                 + [jax.ShapeDtypeStruct((bsz, n, 256), BF16)] * 6)
    out_specs = ([pl.BlockSpec((1, tm, d_lru), tok)] * 2
                 + [pl.BlockSpec((1, tm, 256), tok)] * 6)
    return pl.pallas_call(
        functools.partial(_inproj_kernel, rope=rope, d_lru=d_lru),
        out_shape=out_shape,
        grid=(bsz, n // tm),
        in_specs=in_specs,
        out_specs=out_specs,
        compiler_params=_cparams("parallel", "parallel"),
        name="in_projection_lat" if rope else "in_projection_ctx",
    )(*args)


def _lru_kernel(*refs, reverse, combine, tl):
    if combine:
        (rp_ref, rc_ref, rn_ref, cw_ref, cb_ref, wg_ref, bg_ref, lam_ref, h0_ref, ho_ref, gate_ref,
         y_ref, last_ref, a_scr, d_scr, carry_scr) = refs
    else:
        (rp_ref, rc_ref, rn_ref, cw_ref, cb_ref, wg_ref, bg_ref, lam_ref, h0_ref,
         y_ref, last_ref, a_scr, d_scr, carry_scr) = refs
    j = pl.program_id(1)
    nt = pl.num_programs(1)
    t_idx = (nt - 1 - j) if reverse else j
    d_lru = rc_ref.shape[-1]

    @pl.when(j == 0)
    def _():
        carry_scr[...] = jnp.broadcast_to(h0_ref[0], carry_scr.shape)

    has_prev = (t_idx > 0).astype(F32)
    has_next = (t_idx < nt - 1).astype(F32)
    ext = jnp.concatenate([rp_ref[0] * has_prev, rc_ref[0], rn_ref[0] * has_next], axis=0)
    cw = cw_ref[...]
    u = cb_ref[...] + cw[0:1] * ext[6:6 + tl]
    u = u + cw[1:2] * ext[7:7 + tl]
    u = u + cw[2:3] * ext[8:8 + tl]
    u = u + cw[3:4] * ext[9:9 + tl]

    gts = _dot(u.astype(BF16), wg_ref[...]) + bg_ref[...]
    r_g = jax.nn.sigmoid(gts[:, :d_lru])
    i_g = jax.nn.sigmoid(gts[:, d_lru:])
    nlam = -lam_ref[...]
    sp = jnp.maximum(nlam, 0.0) + jnp.log1p(jnp.exp(-jnp.abs(nlam)))
    log_a = (-LRU_C) * r_g * sp
    a = jnp.exp(log_a)
    a_scr[...] = a
    d_scr[...] = jnp.sqrt(1.0 - a * a) * i_g * u

    row = lax.broadcasted_iota(jnp.int32, (SUBLANES, d_lru), 0)
    n_chunks = tl // SUBLANES

    def chunk(c, carry):
        cc = (n_chunks - 1 - c) if reverse else c
        off = pl.multiple_of(cc * SUBLANES, SUBLANES)
        av = a_scr[pl.ds(off, SUBLANES), :]
        dv = d_scr[pl.ds(off, SUBLANES), :]
        for s in (1, 2, 4):
            if reverse:
                a_sh = pltpu.roll(av, SUBLANES - s, axis=0)
                d_sh = pltpu.roll(dv, SUBLANES - s, axis=0)
                ok = row < SUBLANES - s
            else:
                a_sh = pltpu.roll(av, s, axis=0)
                d_sh = pltpu.roll(dv, s, axis=0)
                ok = row >= s
            dv = jnp.where(ok, av * d_sh + dv, dv)
            av = jnp.where(ok, av * a_sh, av)
        hv = av * carry + dv
        d_scr[pl.ds(off, SUBLANES), :] = hv
        edge = hv[0:1, :] if reverse else hv[SUBLANES - 1:SUBLANES, :]
        return jnp.broadcast_to(edge, (SUBLANES, d_lru))

    carry = lax.fori_loop(0, n_chunks, chunk, carry_scr[...])
    carry_scr[...] = carry
    last_ref[0] = carry[0:1, :]

    hcur = d_scr[...]
    if combine:
        y_ref[0] = ((hcur + ho_ref[0]) * _gelu_tanh(gate_ref[0])).astype(y_ref.dtype)
    else:
        y_ref[0] = hcur


def _lru_pass(r, conv_w, conv_b, wg, bg, lam, h0, other, gate, reverse, tl):
    bsz, n, c = r.shape
    nt = n // tl
    hb = tl // SUBLANES
    nhb = n // SUBLANES
    combine = other is not None
    tt = (lambda j: nt - 1 - j) if reverse else (lambda j: j)
    cur = lambda b, j: (b, tt(j), 0)
    prev = lambda b, j: (b, jnp.maximum(tt(j) * hb - 1, 0), 0)
    nxt = lambda b, j: (b, jnp.minimum((tt(j) + 1) * hb, nhb - 1), 0)
    const2 = lambda b, j: (0, 0)
    in_specs = [pl.BlockSpec((1, SUBLANES, c), prev),
                pl.BlockSpec((1, tl, c), cur),
                pl.BlockSpec((1, SUBLANES, c), nxt),
                pl.BlockSpec(conv_w.shape, const2),
                pl.BlockSpec((1, c), const2),
                pl.BlockSpec(wg.shape, const2),
                pl.BlockSpec((1, 2 * c), const2),
                pl.BlockSpec((1, c), const2),
                pl.BlockSpec((1, 1, c), lambda b, j: (b, 0, 0))]
    args = [r, r, r, conv_w, conv_b, wg, bg, lam, h0]
    if combine:
        in_specs += [pl.BlockSpec((1, tl, c), cur), pl.BlockSpec((1, tl, c), cur)]
        args += [other, gate]
    y_dtype = BF16 if combine else F32
    return pl.pallas_call(
        functools.partial(_lru_kernel, reverse=reverse, combine=combine, tl=tl),
        out_shape=[jax.ShapeDtypeStruct((bsz, n, c), y_dtype),
                   jax.ShapeDtypeStruct((bsz, 1, c), F32)],
        grid=(bsz, nt),
        in_specs=in_specs,
        out_specs=[pl.BlockSpec((1, tl, c), cur),
                   pl.BlockSpec((1, 1, c), lambda b, j: (b, 0, 0))],
        scratch_shapes=[pltpu.VMEM((tl, c), F32), pltpu.VMEM((tl, c), F32),
                        pltpu.VMEM((SUBLANES, c), F32)],
        compiler_params=_cparams("parallel", "arbitrary"),
        name=("lru_bwd" if reverse else "lru_fwd") + ("_combine" if combine else ""),
    )(*args)


def _stack_heads(qg):
    lane = lax.broadcasted_iota(jnp.int32, qg.shape, 1)
    lo = lane < HEAD_DIM
    zero = jnp.zeros_like(qg)
    return jnp.concatenate([jnp.where(lo, qg, zero), jnp.where(lo, zero, qg)], axis=0)


def _unstack_heads(o, tq):
    lane = lax.broadcasted_iota(jnp.int32, (tq, LANES), 1)
    return jnp.where(lane < HEAD_DIM, o[:tq], o[tq:])


def _gattn_kernel(q_ref, kl_ref, vl_ref, kc_ref, vc_ref, o_ref, *, tk):
    tq = q_ref.shape[1]
    n_lat = kl_ref.shape[1]
    for g in range(2):
        gs = slice(LANES * g, LANES * (g + 1))
        q2 = _stack_heads(q_ref[0, :, gs])

        def step(k, v, carry):
            m, l, acc = carry
            s = _dot_nt(q2, k)
            m_new = jnp.maximum(m, jnp.max(s, axis=1, keepdims=True))
            alpha = jnp.exp(m - m_new)
            p = jnp.exp(s - m_new)
            l = alpha * l + jnp.sum(p, axis=1, keepdims=True)
            acc = alpha * acc + _dot(p.astype(BF16), v)
            return m_new, l, acc

        def body(c, carry):
            off = pl.multiple_of(c * tk, tk)
            return step(kl_ref[0, pl.ds(off, tk), gs], vl_ref[0, pl.ds(off, tk), gs], carry)

        init = (jnp.full((2 * tq, 1), NEG_INF, F32), jnp.zeros((2 * tq, 1), F32),
                jnp.zeros((2 * tq, LANES), F32))
        carry = lax.fori_loop(0, n_lat // tk, body, init)
        _, l, acc = step(kc_ref[0, :, gs], vc_ref[0, :, gs], carry)
        o_ref[0, :, gs] = _unstack_heads(acc / l, tq).astype(o_ref.dtype)


def _global_attention(q, k_lat, v_lat, k_ctx, v_ctx, tq, tk):
    bsz, n, _ = q.shape
    n_ctx = k_ctx.shape[1]
    full = lambda b, i: (b, 0, 0)
    return pl.pallas_call(
        functools.partial(_gattn_kernel, tk=tk),
        out_shape=jax.ShapeDtypeStruct((bsz, n, 256), BF16),
        grid=(bsz, n // tq),
        in_specs=[pl.BlockSpec((1, tq, 256), lambda b, i: (b, i, 0)),
                  pl.BlockSpec((1, n, 256), full), pl.BlockSpec((1, n, 256), full),
                  pl.BlockSpec((1, n_ctx, 256), full), pl.BlockSpec((1, n_ctx, 256), full)],
        out_specs=pl.BlockSpec((1, tq, 256), lambda b, i: (b, i, 0)),
        compiler_params=_cparams("parallel", "arbitrary"),
        name="global_attention",
    )(q, k_lat, v_lat, k_ctx, v_ctx)


def _sink_column(sink_ref, g, rows):
    row = lax.broadcasted_iota(jnp.int32, (2 * rows, 1), 0)
    return jnp.where(row < rows, sink_ref[2 * g], sink_ref[2 * g + 1])


def _wattn_kernel(sink_ref, q_ref, kp_ref, kc_ref, kn_ref, vp_ref, vc_ref, vn_ref, kx_ref, vx_ref, o_ref):
    i = pl.program_id(1)
    nb = pl.num_programs(1)
    tq = q_ref.shape[1]
    rowq = lax.broadcasted_iota(jnp.int32, (2 * tq, 3 * tq), 0) % tq
    col = lax.broadcasted_iota(jnp.int32, (2 * tq, 3 * tq), 1)
    valid = (col >= rowq) & (col <= rowq + 2 * WINDOW)
    valid &= (col >= tq) | (i > 0)
    valid &= (col < 2 * tq) | (i < nb - 1)
    for g in range(2):
        gs = slice(LANES * g, LANES * (g + 1))
        q2 = _stack_heads(q_ref[0, :, gs])
        k_loc = jnp.concatenate([kp_ref[0, :, gs], kc_ref[0, :, gs], kn_ref[0, :, gs]], axis=0)
        v_loc = jnp.concatenate([vp_ref[0, :, gs], vc_ref[0, :, gs], vn_ref[0, :, gs]], axis=0)
        s_loc = jnp.where(valid, _dot_nt(q2, k_loc), NEG_INF)
        s_ctx = _dot_nt(q2, kx_ref[0, :, gs])
        s_sink = _sink_column(sink_ref, g, tq)
        m = jnp.maximum(jnp.maximum(jnp.max(s_loc, axis=1, keepdims=True),
                                    jnp.max(s_ctx, axis=1, keepdims=True)), s_sink)
        p_loc = jnp.exp(s_loc - m)
        p_ctx = jnp.exp(s_ctx - m)
        den = (jnp.sum(p_loc, axis=1, keepdims=True) + jnp.sum(p_ctx, axis=1, keepdims=True)
               + jnp.exp(s_sink - m))
        acc = _dot(p_loc.astype(BF16), v_loc) + _dot(p_ctx.astype(BF16), vx_ref[0, :, gs])
        o_ref[0, :, gs] = _unstack_heads(acc / den, tq).astype(o_ref.dtype)


def _window_attention(sink, q, k, v, k_ctx, v_ctx):
    bsz, n, _ = q.shape
    n_ctx = k_ctx.shape[1]
    nb = n // Q_BLOCK
    cur = lambda b, i: (b, i, 0)
    prev = lambda b, i: (b, jnp.maximum(i - 1, 0), 0)
    nxt = lambda b, i: (b, jnp.minimum(i + 1, nb - 1), 0)
    full = lambda b, i: (b, 0, 0)
    blk = lambda im: pl.BlockSpec((1, Q_BLOCK, 256), im)
    return pl.pallas_call(
        _wattn_kernel,
        out_shape=jax.ShapeDtypeStruct((bsz, n, 256), BF16),
        grid=(bsz, nb),
        in_specs=[pl.BlockSpec(memory_space=pltpu.SMEM),
                  blk(cur), blk(prev), blk(cur), blk(nxt), blk(prev), blk(cur), blk(nxt),
                  pl.BlockSpec((1, n_ctx, 256), full), pl.BlockSpec((1, n_ctx, 256), full)],
        out_specs=blk(cur),
        compiler_params=_cparams("parallel", "arbitrary"),
        name="window_attention",
    )(sink, q, k, k, k, v, v, v, k_ctx, v_ctx)


def _cattn_kernel(sink_ref, gq_ref, gk_ref, gv_ref, wq_ref, wk_ref, wv_ref, yb_ref, yc_ref):
    tq = gq_ref.shape[1]
    for g in range(2):
        gs = slice(LANES * g, LANES * (g + 1))
        s = _dot_nt(_stack_heads(gq_ref[0, :, gs]), gk_ref[0, :, gs])
        p = jnp.exp(s - jnp.max(s, axis=1, keepdims=True))
        o = _dot(p.astype(BF16), gv_ref[0, :, gs]) / jnp.sum(p, axis=1, keepdims=True)
        yb_ref[0, :, gs] = _unstack_heads(o, tq).astype(yb_ref.dtype)
        s = _dot_nt(_stack_heads(wq_ref[0, :, gs]), wk_ref[0, :, gs])
        s_sink = _sink_column(sink_ref, g, tq)
        m = jnp.maximum(jnp.max(s, axis=1, keepdims=True), s_sink)
        p = jnp.exp(s - m)
        den = jnp.sum(p, axis=1, keepdims=True) + jnp.exp(s_sink - m)
        o = _dot(p.astype(BF16), wv_ref[0, :, gs]) / den
        yc_ref[0, :, gs] = _unstack_heads(o, tq).astype(yc_ref.dtype)


def _context_attention(sink, gq, gk, gv, wq, wk, wv):
    bsz, n_ctx, _ = gq.shape
    blk = pl.BlockSpec((1, n_ctx, 256), lambda b: (b, 0, 0))
    return pl.pallas_call(
        _cattn_kernel,
        out_shape=[jax.ShapeDtypeStruct((bsz, n_ctx, 256), BF16)] * 2,
        grid=(bsz,),
        in_specs=[pl.BlockSpec(memory_space=pltpu.SMEM)] + [blk] * 6,
        out_specs=[blk, blk],
        compiler_params=_cparams("parallel"),
        name="context_attention",
    )(sink, gq, gk, gv, wq, wk, wv)


def _outproj_kernel(h_ref, ya_ref, yb_ref, yc_ref, wa_ref, wb_ref, wc_ref, mod_ref, g_ref, h1_ref, a2_ref):
    proj = _dot(ya_ref[0], wa_ref[...]) + _dot(yb_ref[0], wb_ref[...]) + _dot(yc_ref[0], wc_ref[...])
    h1 = h_ref[0] + mod_ref[0, 2:3, :] * proj
    h1_ref[0] = h1
    a2 = _rms_rows(h1, g_ref[...]) * (1.0 + mod_ref[0, 4:5, :]) + mod_ref[0, 3:4, :]
    a2_ref[0] = a2.astype(a2_ref.dtype)


def _out_projection(h, ya, yb, yc, w_out, mod, mod_row, g2, tm):
    bsz, n, d = h.shape
    ca, cb, cc = ya.shape[-1], yb.shape[-1], yc.shape[-1]
    tok = lambda b, i: (b, i, 0)
    const2 = lambda b, i: (0, 0)
    return pl.pallas_call(
        _outproj_kernel,
        out_shape=[jax.ShapeDtypeStruct((bsz, n, d), F32), jax.ShapeDtypeStruct((bsz, n, d), BF16)],
        grid=(bsz, n // tm),
        in_specs=[pl.BlockSpec((1, tm, d), tok),
                  pl.BlockSpec((1, tm, ca), tok), pl.BlockSpec((1, tm, cb), tok), pl.BlockSpec((1, tm, cc), tok),
                  pl.BlockSpec((ca, d), const2), pl.BlockSpec((cb, d), const2), pl.BlockSpec((cc, d), const2),
                  pl.BlockSpec((1, 6, d), lambda b, i: (mod_row(b), 0, 0)),
                  pl.BlockSpec((1, d), const2)],
        out_specs=[pl.BlockSpec((1, tm, d), tok), pl.BlockSpec((1, tm, d), tok)],
        compiler_params=_cparams("parallel", "parallel"),
        name="out_projection",
    )(h, ya, yb, yc, w_out[:ca], w_out[ca:ca + cb], w_out[ca + cb:], mod, g2)


def _ffn_kernel(ap_ref, ac_ref, an_ref, h1_ref, wu_ref, cw_ref, cb_ref, wd_ref, mod_ref, o_ref, acc_scr, *, tm, ck):
    i = pl.program_id(1)
    nt = pl.num_programs(1)
    prev = jnp.where(i > 0, ap_ref[0], jnp.zeros_like(ap_ref[0]))
    nxt = jnp.where(i < nt - 1, an_ref[0], jnp.zeros_like(an_ref[0]))
    ext = jnp.concatenate([prev, ac_ref[0], nxt], axis=0)
    acc_scr[...] = jnp.zeros_like(acc_scr)
    n_chunks = wu_ref.shape[0]
    h0 = BF16_ROWS

    def chunk(c, _):
        up = _dot(ext, wu_ref[c])
        cw = cw_ref[c]
        u = cb_ref[c] + cw[0:1] * up[h0 - 1:h0 - 1 + tm] + cw[1:2] * up[h0:h0 + tm] + cw[2:3] * up[h0 + 1:h0 + 1 + tm]
        act = _silu(u[:, :ck]) * u[:, ck:]
        acc_scr[...] += _dot(act.astype(BF16), wd_ref[c])
        return 0

    lax.fori_loop(0, n_chunks, chunk, 0)
    o_ref[0] = h1_ref[0] + mod_ref[0, 5:6, :] * acc_scr[...]


def _conv_ffn(a2, h1, wu, cw, cb, wd, mod, mod_row, tm):
    bsz, n, d = h1.shape
    n_chunks, _, ck2 = wu.shape
    ck = ck2 // 2
    hb = tm // BF16_ROWS
    nhb = n // BF16_ROWS
    tok = lambda b, i: (b, i, 0)
    prev = lambda b, i: (b, jnp.maximum(i * hb - 1, 0), 0)
    nxt = lambda b, i: (b, jnp.minimum((i + 1) * hb, nhb - 1), 0)
    const3 = lambda b, i: (0, 0, 0)
    single = pl.Buffered(1)
    return pl.pallas_call(
        functools.partial(_ffn_kernel, tm=tm, ck=ck),
        out_shape=jax.ShapeDtypeStruct((bsz, n, d), F32),
        grid=(bsz, n // tm),
        in_specs=[pl.BlockSpec((1, BF16_ROWS, d), prev),
                  pl.BlockSpec((1, tm, d), tok),
                  pl.BlockSpec((1, BF16_ROWS, d), nxt),
                  pl.BlockSpec((1, tm, d), tok),
                  pl.BlockSpec(wu.shape, const3, pipeline_mode=single),
                  pl.BlockSpec(cw.shape, const3),
                  pl.BlockSpec(cb.shape, const3),
                  pl.BlockSpec(wd.shape, const3, pipeline_mode=single),
                  pl.BlockSpec((1, 6, d), lambda b, i: (mod_row(b), 0, 0))],
        out_specs=pl.BlockSpec((1, tm, d), tok),
        scratch_shapes=[pltpu.VMEM((tm, d), F32)],
        compiler_params=_cparams("parallel", "parallel"),
        name="conv_ffn",
    )(a2, a2, a2, h1, wu, cw, cb, wd, mod)


def _final_norm_kernel(h_ref, g_ref, o_ref):
    o_ref[0] = _rms_rows(h_ref[0], g_ref[...])


def _final_norm(h, g, tm):
    bsz, n, d = h.shape
    tok = lambda b, i: (b, i, 0)
    return pl.pallas_call(
        _final_norm_kernel,
        out_shape=jax.ShapeDtypeStruct((bsz, n, d), F32),
        grid=(bsz, n // tm),
        in_specs=[pl.BlockSpec((1, tm, d), tok), pl.BlockSpec((1, d), lambda b, i: (0, 0))],
        out_specs=pl.BlockSpec((1, tm, d), tok),
        compiler_params=_cparams("parallel", "parallel"),
        name="final_norm",
    )(h, g)


def _rope_tables(n):
    pos = jnp.arange(n)
    row = (pos // GRID_W).astype(F32)
    col = (pos % GRID_W).astype(F32)
    n_freq = HEAD_DIM // 4
    inv = ROPE_BASE ** (-jnp.arange(n_freq, dtype=F32) / n_freq)
    ang = jnp.concatenate([row[:, None] * inv, col[:, None] * inv], axis=-1)
    cos, sin = jnp.cos(ang), jnp.sin(ang)
    cos_h = jnp.concatenate([cos, cos], axis=-1)
    sin_h = jnp.concatenate([-sin, sin], axis=-1)
    return jnp.tile(cos_h, (1, 4)), jnp.tile(sin_h, (1, 4))


def _block_diag(w):
    nb, c, d = w.shape
    eye = jnp.eye(nb, dtype=w.dtype)
    return jnp.einsum("ncd,nm->ncmd", w, eye).reshape(nb * c, nb * d)


def _ffn_chunks(d_ff):
    for ck in (256, 128):
        if d_ff % ck == 0:
            return ck
    raise ValueError("d_ff must be a multiple of 128")


def kernel(x, c, ctx, c_ctx, w_mod, b_mod, norm1_g, w_in, lru_conv_w, lru_conv_b, lru_w_a, lru_b_a, lru_w_x, lru_b_x, lru_lam, ga_q_norm_g, ga_k_norm_g, wa_sink, w_out, norm2_g, w_up, ffn_conv_w, ffn_conv_b, w_down, final_norm_g):
    bsz, n, d = x.shape
    n_ctx = ctx.shape[1]
    depth = w_in.shape[0]
    d_lru = lru_conv_w.shape[-1]
    d_ff = w_down.shape[1]
    assert w_in.shape[2] == 2 * d_lru + 1024 and n % 512 == 0 and n_ctx % 128 == 0 and bsz <= 16

    rows = 24
    cvec = jnp.zeros((rows, d), F32).at[:bsz].set(c).at[bsz].set(c_ctx)
    mod_all = _modulation(cvec, w_mod, b_mod).reshape(depth, rows, 6, d)
    lat_row = lambda b: b
    ctx_row = lambda b: bsz

    cos, sin = _rope_tables(n)
    seg = _block_diag(jnp.ones((4, HEAD_DIM, HEAD_DIM), BF16))
    ck = _ffn_chunks(d_ff)
    n_ck = d_ff // ck

    tm_lat, tm_ctx = 512, n_ctx
    h, hc = x, ctx
    for l in range(depth):
        need_ctx = l < depth - 1
        mod = mod_all[l]
        g1 = norm1_g[l].reshape(1, d)
        g2 = norm2_g[l].reshape(1, d)
        w_in_l = w_in[l].astype(BF16)
        qg = jnp.tile(ga_q_norm_g[l], 4).reshape(1, 256)
        kg = jnp.tile(ga_k_norm_g[l], 2).reshape(1, 128)
        sink = wa_sink[l]
        conv_w = lru_conv_w[l]
        conv_b = lru_conv_b[l].reshape(1, d_lru)
        w_out_l = w_out[l].astype(BF16)
        wu = w_up[l].astype(BF16)
        wu = jnp.concatenate([wu[:, :d_ff].reshape(d, n_ck, ck), wu[:, d_ff:].reshape(d, n_ck, ck)], axis=-1)
        wu = wu.transpose(1, 0, 2)
        fcw = ffn_conv_w[l]
        fcw = jnp.concatenate([fcw[:, :d_ff].reshape(-1, n_ck, ck), fcw[:, d_ff:].reshape(-1, n_ck, ck)], axis=-1)
        fcw = fcw.transpose(1, 0, 2)
        fcb = ffn_conv_b[l]
        fcb = jnp.concatenate([fcb[:d_ff].reshape(n_ck, 1, ck), fcb[d_ff:].reshape(n_ck, 1, ck)], axis=-1)
        wd = w_down[l].astype(BF16).reshape(n_ck, ck, d)

        r_c, gate_c, gq_c, gk_c, gv_c, wq_c, wk_c, wv_c = _in_projection(
            hc, mod, ctx_row, g1, w_in_l, seg, qg, kg, None, None, tm_ctx)
        r_l, gate_l, gq_l, gk_l, gv_l, wq_l, wk_l, wv_l = _in_projection(
            h, mod, lat_row, g1, w_in_l, seg, qg, kg, cos, sin, tm_lat)

        wg = [jnp.concatenate([_block_diag(lru_w_a[l, dr]), _block_diag(lru_w_x[l, dr])], axis=1).astype(BF16)
              for dr in range(2)]
        bg = [jnp.concatenate([lru_b_a[l, dr], lru_b_x[l, dr]]).reshape(1, 2 * d_lru) for dr in range(2)]
        lam = [lru_lam[l, dr].reshape(1, d_lru) for dr in range(2)]
        zero_state = jnp.zeros((bsz, 1, d_lru), F32)
        hf_c, last_f = _lru_pass(r_c, conv_w, conv_b, wg[0], bg[0], lam[0], zero_state, None, None, False, n_ctx)
        ya_c, last_b = _lru_pass(r_c, conv_w, conv_b, wg[1], bg[1], lam[1], zero_state, hf_c, gate_c, True, n_ctx)
        hf_l, _ = _lru_pass(r_l, conv_w, conv_b, wg[0], bg[0], lam[0], last_f, None, None, False, 512)
        ya_l, _ = _lru_pass(r_l, conv_w, conv_b, wg[1], bg[1], lam[1], last_b, hf_l, gate_l, True, 512)

        yb_l = _global_attention(gq_l, gk_l, gv_l, gk_c, gv_c, 256, 512)
        yc_l = _window_attention(sink, wq_l, wk_l, wv_l, wk_c, wv_c)

        h1, a2 = _out_projection(h, ya_l, yb_l, yc_l, w_out_l, mod, lat_row, g2, tm_lat)
        h = _conv_ffn(a2, h1, wu, fcw, fcb, wd, mod, lat_row, tm_lat)
        if need_ctx:
            yb_c, yc_c = _context_attention(sink, gq_c, gk_c, gv_c, wq_c, wk_c, wv_c)
            hc1, ac2 = _out_projection(hc, ya_c, yb_c, yc_c, w_out_l, mod, ctx_row, g2, tm_ctx)
            hc = _conv_ffn(ac2, hc1, wu, fcw, fcb, wd, mod, ctx_row, tm_ctx)
    return _final_norm(h, final_norm_g.reshape(1, d), tm_lat)
```

```python
import functools

import jax
import jax.numpy as jnp
from jax import lax
from jax.experimental import pallas as pl
from jax.experimental.pallas import tpu as pltpu

F32 = jnp.float32
BF16 = jnp.bfloat16

HEAD_DIM = 64
GRID_W = 64
WINDOW = 128
Q_BLOCK = 128
LRU_C = 8.0
ROPE_BASE = 10000.0
EPS = 1e-6
NEG_INF = -1e30
ATTN_SCALE = HEAD_DIM ** -0.5
LOG2E = 1.4426950408889634
Q_SCALE = ATTN_SCALE * LOG2E
LANES = 128
SUBLANES = 8
BF16_ROWS = 16
VMEM_LIMIT = 56 * 1024 * 1024


def _cparams(*sem):
    return pltpu.CompilerParams(dimension_semantics=sem, vmem_limit_bytes=VMEM_LIMIT)


def _rms_rows(x, g):
    ms = jnp.mean(x * x, axis=-1, keepdims=True)
    return x * lax.rsqrt(ms + EPS) * g


def _silu(x):
    return x * jax.nn.sigmoid(x)


def _gelu_tanh(x):
    c = 0.7978845608028654
    return 0.5 * x * (1.0 + jnp.tanh(c * (x + 0.044715 * (x * x * x))))


def _dot(a, b):
    return jnp.dot(a, b, preferred_element_type=F32)


def _dot_nt(a, b):
    return lax.dot_general(a, b, (((1,), (1,)), ((), ())), preferred_element_type=F32)


def _mod_kernel(c_ref, w_ref, b_ref, o_ref):
    s = _silu(c_ref[...])
    o_ref[0] = jnp.dot(s, w_ref[0], preferred_element_type=F32,
                       precision=lax.Precision.HIGHEST) + b_ref[0]


def _modulation(cvec, w_mod, b_mod):
    n_layers, d, d6 = w_mod.shape
    rows = cvec.shape[0]
    tn = 1536
    return pl.pallas_call(
        _mod_kernel,
        out_shape=jax.ShapeDtypeStruct((n_layers, rows, d6), F32),
        grid=(n_layers, d6 // tn),
        in_specs=[pl.BlockSpec((rows, d), lambda l, j: (0, 0)),
                  pl.BlockSpec((1, d, tn), lambda l, j: (l, 0, j)),
                  pl.BlockSpec((1, 1, tn), lambda l, j: (l, 0, j))],
        out_specs=pl.BlockSpec((1, rows, tn), lambda l, j: (l, 0, j)),
        compiler_params=_cparams("parallel", "parallel"),
        name="modulation",
    )(cvec, w_mod, b_mod.reshape(n_layers, 1, d6))


def _seg_mean_sq(x, seg):
    sq = x * x
    hi = sq.astype(BF16)
    lo = (sq - hi.astype(F32)).astype(BF16)
    return (_dot(hi, seg) + _dot(lo, seg)) * (1.0 / HEAD_DIM)


def _swap_halves(x):
    w = x.shape[-1]
    lane = lax.broadcasted_iota(jnp.int32, x.shape, 1)
    first = (lane % HEAD_DIM) < (HEAD_DIM // 2)
    up = pltpu.roll(x, w - HEAD_DIM // 2, axis=1)
    down = pltpu.roll(x, HEAD_DIM // 2, axis=1)
    return jnp.where(first, up, down)


def _dup_kv_heads(k):
    lane = lax.broadcasted_iota(jnp.int32, k.shape, 1)
    lo = lane < HEAD_DIM
    sw = pltpu.roll(k, HEAD_DIM, axis=1)
    return jnp.concatenate([jnp.where(lo, k, sw), jnp.where(lo, sw, k)], axis=1)


def _aug_kv_heads(v):
    lane = lax.broadcasted_iota(jnp.int32, v.shape, 1)
    lo = lane < HEAD_DIM
    sw = pltpu.roll(v, HEAD_DIM, axis=1)
    return jnp.concatenate([jnp.where(lo, v, 1.0), jnp.where(lo, sw, 1.0)], axis=1)


def _inproj_kernel(*refs, rope, d_lru):
    if rope:
        (h_ref, mod_ref, g_ref, w_ref, seg_ref, qg_ref, kg_ref, cos_ref, sin_ref,
         r_ref, gate_ref, gq_ref, gk_ref, gv_ref, wq_ref, wk_ref, wv_ref) = refs
    else:
        (h_ref, mod_ref, g_ref, w_ref, seg_ref, qg_ref, kg_ref,
         r_ref, gate_ref, gq_ref, gk_ref, gv_ref, wq_ref, wk_ref, wv_ref) = refs
    x = h_ref[0]
    a = _rms_rows(x, g_ref[...]) * (1.0 + mod_ref[0, 1:2, :]) + mod_ref[0, 0:1, :]
    y = _dot(a.astype(BF16), w_ref[...])
    o = 0
    r_ref[0] = y[:, o:o + d_lru]; o += d_lru
    gate_ref[0] = y[:, o:o + d_lru]; o += d_lru
    gq = y[:, o:o + 256]; o += 256
    gk = y[:, o:o + 128]; o += 128
    gv = y[:, o:o + 128]; o += 128
    wq = y[:, o:o + 256]; o += 256
    wk = y[:, o:o + 128]; o += 128
    wv = y[:, o:o + 128]

    seg = seg_ref[...]
    gq = gq * lax.rsqrt(_seg_mean_sq(gq, seg) + EPS) * qg_ref[...]
    gk = gk * lax.rsqrt(_seg_mean_sq(gk, seg[:128, :128]) + EPS) * kg_ref[...]
    if rope:
        cos = cos_ref[...]
        sin = sin_ref[...]
        gq = gq * cos + _swap_halves(gq) * sin
        wq = wq * cos + _swap_halves(wq) * sin
        gk = gk * cos[:, :128] + _swap_halves(gk) * sin[:, :128]
        wk = wk * cos[:, :128] + _swap_halves(wk) * sin[:, :128]
    gq_ref[0] = (gq * Q_SCALE).astype(BF16)
    wq_ref[0] = (wq * Q_SCALE).astype(BF16)
    gk_ref[0] = _dup_kv_heads(gk).astype(BF16)
    gv_ref[0] = _aug_kv_heads(gv).astype(BF16)
    wk_ref[0] = _dup_kv_heads(wk).astype(BF16)
    wv_ref[0] = _aug_kv_heads(wv).astype(BF16)


def _in_projection(h, mod, mod_row, g1, w_in, seg, qg, kg, cos, sin, tm):
    bsz, n, d = h.shape
    d_in = w_in.shape[1]
    d_lru = (d_in - 1024) // 2
    rope = cos is not None
    tok = lambda b, i: (b, i, 0)
    const2 = lambda b, i: (0, 0)
    in_specs = [pl.BlockSpec((1, tm, d), tok),
                pl.BlockSpec((1, 6, d), lambda b, i: (mod_row(b), 0, 0)),
                pl.BlockSpec((1, d), const2),
                pl.BlockSpec((d, d_in), const2),
                pl.BlockSpec((256, 256), const2),
                pl.BlockSpec((1, 256), const2),
                pl.BlockSpec((1, 128), const2)]
    args = [h, mod, g1, w_in, seg, qg, kg]
    if rope:
        in_specs += [pl.BlockSpec((tm, 256), lambda b, i: (i, 0)),
                     pl.BlockSpec((tm, 256), lambda b, i: (i, 0))]
        args += [cos, sin]
    out_shape = ([jax.ShapeDtypeStruct((bsz, n, d_lru), F32)] * 2
                 + [jax.ShapeDtypeStruct((bsz, n, 256), BF16)] * 6)
    out_specs = ([pl.BlockSpec((1, tm, d_lru), tok)] * 2
                 + [pl.BlockSpec((1, tm, 256), tok)] * 6)
    return pl.pallas_call(
        functools.partial(_inproj_kernel, rope=rope, d_lru=d_lru),
        out_shape=out_shape,
        grid=(bsz, n // tm),
        in_specs=in_specs,
        out_specs=out_specs,
        compiler_params=_cparams("parallel", "parallel"),
        name="in_projection_lat" if rope else "in_projection_ctx",
    )(*args)


def _lru_kernel(*refs, reverse, combine, tl):
    if combine:
        (rp_ref, rc_ref, rn_ref, cw_ref, cb_ref, wg_ref, bg_ref, lam_ref, h0_ref, ho_ref, gate_ref,
         y_ref, last_ref, a_scr, d_scr, carry_scr) = refs
    else:
        (rp_ref, rc_ref, rn_ref, cw_ref, cb_ref, wg_ref, bg_ref, lam_ref, h0_ref,
         y_ref, last_ref, a_scr, d_scr, carry_scr) = refs
    j = pl.program_id(1)
    nt = pl.num_programs(1)
    t_idx = (nt - 1 - j) if reverse else j
    d_lru = rc_ref.shape[-1]

    @pl.when(j == 0)
    def _():
        carry_scr[...] = jnp.broadcast_to(h0_ref[0], carry_scr.shape)

    has_prev = (t_idx > 0).astype(F32)
    has_next = (t_idx < nt - 1).astype(F32)
    ext = jnp.concatenate([rp_ref[0] * has_prev, rc_ref[0], rn_ref[0] * has_next], axis=0)
    cw = cw_ref[...]
    u = cb_ref[...] + cw[0:1] * ext[6:6 + tl]
    u = u + cw[1:2] * ext[7:7 + tl]
    u = u + cw[2:3] * ext[8:8 + tl]
    u = u + cw[3:4] * ext[9:9 + tl]

    gts = _dot(u.astype(BF16), wg_ref[...]) + bg_ref[...]
    r_g = jax.nn.sigmoid(gts[:, :d_lru])
    i_g = jax.nn.sigmoid(gts[:, d_lru:])
    nlam = -lam_ref[...]
    sp = jnp.maximum(nlam, 0.0) + jnp.log1p(jnp.exp(-jnp.abs(nlam)))
    log_a = (-LRU_C) * r_g * sp
    a = jnp.exp(log_a)
    a_scr[...] = a
    d_scr[...] = jnp.sqrt(1.0 - a * a) * i_g * u

    row = lax.broadcasted_iota(jnp.int32, (SUBLANES, d_lru), 0)
    n_chunks = tl // SUBLANES

    def chunk(c, carry):
        cc = (n_chunks - 1 - c) if reverse else c
        off = pl.multiple_of(cc * SUBLANES, SUBLANES)
        av = a_scr[pl.ds(off, SUBLANES), :]
        dv = d_scr[pl.ds(off, SUBLANES), :]
        for s in (1, 2, 4):
            if reverse:
                a_sh = pltpu.roll(av, SUBLANES - s, axis=0)
                d_sh = pltpu.roll(dv, SUBLANES - s, axis=0)
                ok = row < SUBLANES - s
            else:
                a_sh = pltpu.roll(av, s, axis=0)
                d_sh = pltpu.roll(dv, s, axis=0)
                ok = row >= s
            dv = jnp.where(ok, av * d_sh + dv, dv)
            av = jnp.where(ok, av * a_sh, av)
        hv = av * carry + dv
        d_scr[pl.ds(off, SUBLANES), :] = hv
        edge = hv[0:1, :] if reverse else hv[SUBLANES - 1:SUBLANES, :]
        return jnp.broadcast_to(edge, (SUBLANES, d_lru))

    carry = lax.fori_loop(0, n_chunks, chunk, carry_scr[...])
    carry_scr[...] = carry
    last_ref[0] = carry[0:1, :]

    hcur = d_scr[...]
    if combine:
        y_ref[0] = ((hcur + ho_ref[0]) * _gelu_tanh(gate_ref[0])).astype(y_ref.dtype)
    else:
        y_ref[0] = hcur


def _lru_pass(r, conv_w, conv_b, wg, bg, lam, h0, other, gate, reverse, tl):
    bsz, n, c = r.shape
    nt = n // tl
    hb = tl // SUBLANES
    nhb = n // SUBLANES
    combine = other is not None
    tt = (lambda j: nt - 1 - j) if reverse else (lambda j: j)
    cur = lambda b, j: (b, tt(j), 0)
    prev = lambda b, j: (b, jnp.maximum(tt(j) * hb - 1, 0), 0)
    nxt = lambda b, j: (b, jnp.minimum((tt(j) + 1) * hb, nhb - 1), 0)
    const2 = lambda b, j: (0, 0)
    in_specs = [pl.BlockSpec((1, SUBLANES, c), prev),
                pl.BlockSpec((1, tl, c), cur),
                pl.BlockSpec((1, SUBLANES, c), nxt),
                pl.BlockSpec(conv_w.shape, const2),
                pl.BlockSpec((1, c), const2),
                pl.BlockSpec(wg.shape, const2),
                pl.BlockSpec((1, 2 * c), const2),
                pl.BlockSpec((1, c), const2),
                pl.BlockSpec((1, 1, c), lambda b, j: (b, 0, 0))]
    args = [r, r, r, conv_w, conv_b, wg, bg, lam, h0]
    if combine:
        in_specs += [pl.BlockSpec((1, tl, c), cur), pl.BlockSpec((1, tl, c), cur)]
        args += [other, gate]
    y_dtype = BF16 if combine else F32
    return pl.pallas_call(
        functools.partial(_lru_kernel, reverse=reverse, combine=combine, tl=tl),
        out_shape=[jax.ShapeDtypeStruct((bsz, n, c), y_dtype),
                   jax.ShapeDtypeStruct((bsz, 1, c), F32)],
        grid=(bsz, nt),
        in_specs=in_specs,
        out_specs=[pl.BlockSpec((1, tl, c), cur),
                   pl.BlockSpec((1, 1, c), lambda b, j: (b, 0, 0))],
        scratch_shapes=[pltpu.VMEM((tl, c), F32), pltpu.VMEM((tl, c), F32),
                        pltpu.VMEM((SUBLANES, c), F32)],
        compiler_params=_cparams("parallel", "arbitrary"),
        name=("lru_bwd" if reverse else "lru_fwd") + ("_combine" if combine else ""),
    )(*args)


def _stack_heads(qg):
    lane = lax.broadcasted_iota(jnp.int32, qg.shape, 1)
    lo = lane < HEAD_DIM
    zero = jnp.zeros_like(qg)
    return jnp.concatenate([jnp.where(lo, qg, zero), jnp.where(lo, zero, qg)], axis=0)


def _normalise_heads(acc, extra_den, tq):
    den = pltpu.roll(acc, HEAD_DIM, axis=1)
    if extra_den is not None:
        den = den + extra_den
    res = acc / den
    lane = lax.broadcasted_iota(jnp.int32, (tq, LANES), 1)
    return jnp.where(lane < HEAD_DIM, res[:tq], pltpu.roll(res[tq:], HEAD_DIM, axis=1))


def _gattn_kernel(q_ref, kl_ref, vl_ref, kc_ref, vc_ref, o_ref, *, tk):
    tq = q_ref.shape[1]
    n_lat = kl_ref.shape[1]
    n_ctx = kc_ref.shape[1]
    chunks = [(kl_ref, vl_ref, c * tk, tk) for c in range(n_lat // tk)] + [(kc_ref, vc_ref, 0, n_ctx)]
    for g in range(2):
        gs = slice(LANES * g, LANES * (g + 1))
        q2 = _stack_heads(q_ref[0, :, gs])
        m = jnp.full((2 * tq, 1), NEG_INF, F32)
        acc = jnp.zeros((2 * tq, LANES), F32)
        for k_ref, v_ref, off, size in chunks:
            s = _dot_nt(q2, k_ref[0, off:off + size, gs])
            m_new = jnp.maximum(m, jnp.max(s, axis=1, keepdims=True))
            p = jnp.exp2(s - m_new)
            acc = jnp.exp2(m - m_new) * acc + _dot(p.astype(BF16), v_ref[0, off:off + size, gs])
            m = m_new
        o_ref[0, :, gs] = _normalise_heads(acc, None, tq).astype(o_ref.dtype)


def _global_attention(q, k_lat, v_lat, k_ctx, v_ctx, tq, tk):
    bsz, n, _ = q.shape
    n_ctx = k_ctx.shape[1]
    full = lambda b, i: (b, 0, 0)
    return pl.pallas_call(
        functools.partial(_gattn_kernel, tk=tk),
        out_shape=jax.ShapeDtypeStruct((bsz, n, 256), BF16),
        grid=(bsz, n // tq),
        in_specs=[pl.BlockSpec((1, tq, 256), lambda b, i: (b, i, 0)),
                  pl.BlockSpec((1, n, 256), full), pl.BlockSpec((1, n, 256), full),
                  pl.BlockSpec((1, n_ctx, 256), full), pl.BlockSpec((1, n_ctx, 256), full)],
        out_specs=pl.BlockSpec((1, tq, 256), lambda b, i: (b, i, 0)),
        compiler_params=_cparams("parallel", "arbitrary"),
        name="global_attention",
    )(q, k_lat, v_lat, k_ctx, v_ctx)


def _sink_column(sink_ref, g, rows):
    row = lax.broadcasted_iota(jnp.int32, (2 * rows, 1), 0)
    return jnp.where(row < rows, sink_ref[2 * g], sink_ref[2 * g + 1]) * LOG2E


def _wattn_kernel(sink_ref, q_ref, kp_ref, kc_ref, kn_ref, vp_ref, vc_ref, vn_ref, kx_ref, vx_ref, o_ref,
                  kcat, vcat, *, nsub):
    i = pl.program_id(1)
    nb = pl.num_programs(1)
    w = WINDOW
    tq = nsub * w
    kcat[0:w] = kp_ref[0]
    kcat[w:w + tq] = kc_ref[0]
    kcat[w + tq:] = kn_ref[0]
    vcat[0:w] = vp_ref[0]
    vcat[w:w + tq] = vc_ref[0]
    vcat[w + tq:] = vn_ref[0]
    rowq = lax.broadcasted_iota(jnp.int32, (2 * w, 3 * w), 0) % w
    col = lax.broadcasted_iota(jnp.int32, (2 * w, 3 * w), 1)
    band = (col >= rowq) & (col <= rowq + 2 * w)
    for j in range(nsub):
        valid = band
        if j == 0:
            valid = valid & ((col >= w) | (i > 0))
        if j == nsub - 1:
            valid = valid & ((col < 2 * w) | (i < nb - 1))
        rows = slice(j * w, (j + 1) * w)
        keys = slice(j * w, (j + 3) * w)
        for g in range(2):
            gs = slice(LANES * g, LANES * (g + 1))
            q2 = _stack_heads(q_ref[0, rows, gs])
            s_loc = jnp.where(valid, _dot_nt(q2, kcat[keys, gs]), NEG_INF)
            s_ctx = _dot_nt(q2, kx_ref[0, :, gs])
            s_sink = _sink_column(sink_ref, g, w)
            m = jnp.maximum(jnp.maximum(jnp.max(s_loc, axis=1, keepdims=True),
                                        jnp.max(s_ctx, axis=1, keepdims=True)), s_sink)
            p_loc = jnp.exp2(s_loc - m)
            p_ctx = jnp.exp2(s_ctx - m)
            acc = _dot(p_loc.astype(BF16), vcat[keys, gs]) + _dot(p_ctx.astype(BF16), vx_ref[0, :, gs])
            o_ref[0, rows, gs] = _normalise_heads(acc, jnp.exp2(s_sink - m), w).astype(o_ref.dtype)


def _window_attention(sink, q, k, v, k_ctx, v_ctx, nsub):
    bsz, n, _ = q.shape
    n_ctx = k_ctx.shape[1]
    tq = nsub * Q_BLOCK
    nb128 = n // Q_BLOCK
    cur = lambda b, i: (b, i, 0)
    prev = lambda b, i: (b, jnp.maximum(i * nsub - 1, 0), 0)
    nxt = lambda b, i: (b, jnp.minimum((i + 1) * nsub, nb128 - 1), 0)
    full = lambda b, i: (b, 0, 0)
    halo = lambda im: pl.BlockSpec((1, Q_BLOCK, 256), im)
    big = pl.BlockSpec((1, tq, 256), cur)
    return pl.pallas_call(
        functools.partial(_wattn_kernel, nsub=nsub),
        out_shape=jax.ShapeDtypeStruct((bsz, n, 256), BF16),
        grid=(bsz, n // tq),
        in_specs=[pl.BlockSpec(memory_space=pltpu.SMEM),
                  big, halo(prev), big, halo(nxt), halo(prev), big, halo(nxt),
                  pl.BlockSpec((1, n_ctx, 256), full), pl.BlockSpec((1, n_ctx, 256), full)],
        out_specs=big,
        scratch_shapes=[pltpu.VMEM((tq + 2 * Q_BLOCK, 256), BF16), pltpu.VMEM((tq + 2 * Q_BLOCK, 256), BF16)],
        compiler_params=_cparams("parallel", "arbitrary"),
        name="window_attention",
    )(sink, q, k, k, k, v, v, v, k_ctx, v_ctx)


def _cattn_kernel(sink_ref, gq_ref, gk_ref, gv_ref, wq_ref, wk_ref, wv_ref, yb_ref, yc_ref):
    tq = gq_ref.shape[1]
    for g in range(2):
        gs = slice(LANES * g, LANES * (g + 1))
        s = _dot_nt(_stack_heads(gq_ref[0, :, gs]), gk_ref[0, :, gs])
        p = jnp.exp2(s - jnp.max(s, axis=1, keepdims=True))
        acc = _dot(p.astype(BF16), gv_ref[0, :, gs])
        yb_ref[0, :, gs] = _normalise_heads(acc, None, tq).astype(yb_ref.dtype)
        s = _dot_nt(_stack_heads(wq_ref[0, :, gs]), wk_ref[0, :, gs])
        s_sink = _sink_column(sink_ref, g, tq)
        m = jnp.maximum(jnp.max(s, axis=1, keepdims=True), s_sink)
        p = jnp.exp2(s - m)
        acc = _dot(p.astype(BF16), wv_ref[0, :, gs])
        yc_ref[0, :, gs] = _normalise_heads(acc, jnp.exp2(s_sink - m), tq).astype(yc_ref.dtype)


def _context_attention(sink, gq, gk, gv, wq, wk, wv):
    bsz, n_ctx, _ = gq.shape
    blk = pl.BlockSpec((1, n_ctx, 256), lambda b: (b, 0, 0))
    return pl.pallas_call(
        _cattn_kernel,
        out_shape=[jax.ShapeDtypeStruct((bsz, n_ctx, 256), BF16)] * 2,
        grid=(bsz,),
        in_specs=[pl.BlockSpec(memory_space=pltpu.SMEM)] + [blk] * 6,
        out_specs=[blk, blk],
        compiler_params=_cparams("parallel"),
        name="context_attention",
    )(sink, gq, gk, gv, wq, wk, wv)


def _outproj_kernel(h_ref, ya_ref, yb_ref, yc_ref, wa_ref, wb_ref, wc_ref, mod_ref, g_ref, h1_ref, a2_ref):
    proj = _dot(ya_ref[0], wa_ref[...]) + _dot(yb_ref[0], wb_ref[...]) + _dot(yc_ref[0], wc_ref[...])
    h1 = h_ref[0] + mod_ref[0, 2:3, :] * proj
    h1_ref[0] = h1
    a2 = _rms_rows(h1, g_ref[...]) * (1.0 + mod_ref[0, 4:5, :]) + mod_ref[0, 3:4, :]
    a2_ref[0] = a2.astype(a2_ref.dtype)


def _out_projection(h, ya, yb, yc, w_out, mod, mod_row, g2, tm):
    bsz, n, d = h.shape
    ca, cb, cc = ya.shape[-1], yb.shape[-1], yc.shape[-1]
    tok = lambda b, i: (b, i, 0)
    const2 = lambda b, i: (0, 0)
    return pl.pallas_call(
        _outproj_kernel,
        out_shape=[jax.ShapeDtypeStruct((bsz, n, d), F32), jax.ShapeDtypeStruct((bsz, n, d), BF16)],
        grid=(bsz, n // tm),
        in_specs=[pl.BlockSpec((1, tm, d), tok),
                  pl.BlockSpec((1, tm, ca), tok), pl.BlockSpec((1, tm, cb), tok), pl.BlockSpec((1, tm, cc), tok),
                  pl.BlockSpec((ca, d), const2), pl.BlockSpec((cb, d), const2), pl.BlockSpec((cc, d), const2),
                  pl.BlockSpec((1, 6, d), lambda b, i: (mod_row(b), 0, 0)),
                  pl.BlockSpec((1, d), const2)],
        out_specs=[pl.BlockSpec((1, tm, d), tok), pl.BlockSpec((1, tm, d), tok)],
        compiler_params=_cparams("parallel", "parallel"),
        name="out_projection",
    )(h, ya, yb, yc, w_out[:ca], w_out[ca:ca + cb], w_out[ca + cb:], mod, g2)


def _ffn_kernel(*refs, tm, ck, final):
    if final:
        (ap_ref, ac_ref, an_ref, h1_ref, wu_ref, cw_ref, cb_ref, wd_ref, mod_ref, gf_ref,
         o_ref, ext_scr, up_scr, act_scr) = refs
    else:
        (ap_ref, ac_ref, an_ref, h1_ref, wu_ref, cw_ref, cb_ref, wd_ref, mod_ref,
         o_ref, ext_scr, up_scr, act_scr) = refs
    i = pl.program_id(1)
    nt = pl.num_programs(1)
    h0 = BF16_ROWS
    ext_scr[0:h0] = jnp.where(i > 0, ap_ref[0], jnp.zeros_like(ap_ref[0]))
    ext_scr[h0:h0 + tm] = ac_ref[0]
    ext_scr[h0 + tm:] = jnp.where(i < nt - 1, an_ref[0], jnp.zeros_like(an_ref[0]))
    n_chunks = wu_ref.shape[0]

    def up_proj(c, slot):
        up_scr[slot] = _dot(ext_scr[...], wu_ref[c])

    def conv_act(c, slot):
        up = up_scr[slot]
        cw = cw_ref[c]
        u = cb_ref[c] + cw[0:1] * up[h0 - 1:h0 - 1 + tm] + cw[1:2] * up[h0:h0 + tm] + cw[2:3] * up[h0 + 1:h0 + 1 + tm]
        act_scr[c] = (_silu(u[:, :ck]) * u[:, ck:]).astype(BF16)

    up_proj(0, 0)
    n_pairs = (n_chunks - 1) // 2

    def pair(j, _):
        c = 2 * j
        up_proj(c + 1, 1)
        conv_act(c, 0)
        up_proj(c + 2, 0)
        conv_act(c + 1, 1)
        return 0

    for j in range(n_pairs):
        pair(j, 0)
    c = 2 * n_pairs
    if c + 1 < n_chunks:
        up_proj(c + 1, 1)
    conv_act(c, 0)
    if c + 1 < n_chunks:
        conv_act(c + 1, 1)

    acc = _dot(act_scr[0], wd_ref[0])
    for c in range(1, n_chunks):
        acc = acc + _dot(act_scr[c], wd_ref[c])
    h2 = h1_ref[0] + mod_ref[0, 5:6, :] * acc
    if final:
        h2 = _rms_rows(h2, gf_ref[...])
    o_ref[0] = h2


def _conv_ffn(a2, h1, wu, cw, cb, wd, mod, mod_row, final_g, tm):
    bsz, n, d = h1.shape
    n_chunks, _, ck2 = wu.shape
    ck = ck2 // 2
    hb = tm // BF16_ROWS
    nhb = n // BF16_ROWS
    final = final_g is not None
    tok = lambda b, i: (b, i, 0)
    prev = lambda b, i: (b, jnp.maximum(i * hb - 1, 0), 0)
    nxt = lambda b, i: (b, jnp.minimum((i + 1) * hb, nhb - 1), 0)
    const3 = lambda b, i: (0, 0, 0)
    single = pl.Buffered(1)
    in_specs = [pl.BlockSpec((1, BF16_ROWS, d), prev),
                pl.BlockSpec((1, tm, d), tok),
                pl.BlockSpec((1, BF16_ROWS, d), nxt),
                pl.BlockSpec((1, tm, d), tok),
                pl.BlockSpec(wu.shape, const3, pipeline_mode=single),
                pl.BlockSpec(cw.shape, const3),
                pl.BlockSpec(cb.shape, const3),
                pl.BlockSpec(wd.shape, const3, pipeline_mode=single),
                pl.BlockSpec((1, 6, d), lambda b, i: (mod_row(b), 0, 0))]
    args = [a2, a2, a2, h1, wu, cw, cb, wd, mod]
    if final:
        in_specs.append(pl.BlockSpec((1, d), lambda b, i: (0, 0)))
        args.append(final_g)
    return pl.pallas_call(
        functools.partial(_ffn_kernel, tm=tm, ck=ck, final=final),
        out_shape=jax.ShapeDtypeStruct((bsz, n, d), F32),
        grid=(bsz, n // tm),
        in_specs=in_specs,
        out_specs=pl.BlockSpec((1, tm, d), tok),
        scratch_shapes=[pltpu.VMEM((tm + 2 * BF16_ROWS, d), BF16),
                        pltpu.VMEM((2, tm + 2 * BF16_ROWS, ck2), F32),
                        pltpu.VMEM((n_chunks, tm, ck), BF16)],
        compiler_params=_cparams("parallel", "parallel"),
        name="conv_ffn_final" if final else "conv_ffn",
    )(*args)


def _rope_tables(n):
    pos = jnp.arange(n)
    row = (pos // GRID_W).astype(F32)
    col = (pos % GRID_W).astype(F32)
    n_freq = HEAD_DIM // 4
    inv = ROPE_BASE ** (-jnp.arange(n_freq, dtype=F32) / n_freq)
    ang = jnp.concatenate([row[:, None] * inv, col[:, None] * inv], axis=-1)
    cos, sin = jnp.cos(ang), jnp.sin(ang)
    cos_h = jnp.concatenate([cos, cos], axis=-1)
    sin_h = jnp.concatenate([-sin, sin], axis=-1)
    return jnp.tile(cos_h, (1, 4)), jnp.tile(sin_h, (1, 4))


def _block_diag(w):
    nb, c, d = w.shape
    eye = jnp.eye(nb, dtype=w.dtype)
    return jnp.einsum("ncd,nm->ncmd", w, eye).reshape(nb * c, nb * d)


def _ffn_chunks(d_ff):
    for ck in (256, 128):
        if d_ff % ck == 0:
            return ck
    raise ValueError("d_ff must be a multiple of 128")


def kernel(x, c, ctx, c_ctx, w_mod, b_mod, norm1_g, w_in, lru_conv_w, lru_conv_b, lru_w_a, lru_b_a, lru_w_x, lru_b_x, lru_lam, ga_q_norm_g, ga_k_norm_g, wa_sink, w_out, norm2_g, w_up, ffn_conv_w, ffn_conv_b, w_down, final_norm_g):
    bsz, n, d = x.shape
    n_ctx = ctx.shape[1]
    depth = w_in.shape[0]
    d_lru = lru_conv_w.shape[-1]
    d_ff = w_down.shape[1]
    assert w_in.shape[2] == 2 * d_lru + 1024 and n % 512 == 0 and n_ctx % 128 == 0 and bsz <= 16

    rows = 24
    cvec = jnp.zeros((rows, d), F32).at[:bsz].set(c).at[bsz].set(c_ctx)
    mod_all = _modulation(cvec, w_mod, b_mod).reshape(depth, rows, 6, d)
    lat_row = lambda b: b
    ctx_row = lambda b: bsz

    cos, sin = _rope_tables(n)
    seg = _block_diag(jnp.ones((4, HEAD_DIM, HEAD_DIM), BF16))
    ck = _ffn_chunks(d_ff)
    n_ck = d_ff // ck

    tm_lat, tm_ctx = 512, n_ctx
    h, hc = x, ctx
    for l in range(depth):
        need_ctx = l < depth - 1
        mod = mod_all[l]
        g1 = norm1_g[l].reshape(1, d)
        g2 = norm2_g[l].reshape(1, d)
        w_in_l = w_in[l].astype(BF16)
        qg = jnp.tile(ga_q_norm_g[l], 4).reshape(1, 256)
        kg = jnp.tile(ga_k_norm_g[l], 2).reshape(1, 128)
        sink = wa_sink[l]
        conv_w = lru_conv_w[l]
        conv_b = lru_conv_b[l].reshape(1, d_lru)
        w_out_l = w_out[l].astype(BF16)
        wu = w_up[l].astype(BF16)
        wu = jnp.concatenate([wu[:, :d_ff].reshape(d, n_ck, ck), wu[:, d_ff:].reshape(d, n_ck, ck)], axis=-1)
        wu = wu.transpose(1, 0, 2)
        fcw = ffn_conv_w[l]
        fcw = jnp.concatenate([fcw[:, :d_ff].reshape(-1, n_ck, ck), fcw[:, d_ff:].reshape(-1, n_ck, ck)], axis=-1)
        fcw = fcw.transpose(1, 0, 2)
        fcb = ffn_conv_b[l]
        fcb = jnp.concatenate([fcb[:d_ff].reshape(n_ck, 1, ck), fcb[d_ff:].reshape(n_ck, 1, ck)], axis=-1)
        wd = w_down[l].astype(BF16).reshape(n_ck, ck, d)

        r_c, gate_c, gq_c, gk_c, gv_c, wq_c, wk_c, wv_c = _in_projection(
            hc, mod, ctx_row, g1, w_in_l, seg, qg, kg, None, None, tm_ctx)
        r_l, gate_l, gq_l, gk_l, gv_l, wq_l, wk_l, wv_l = _in_projection(
            h, mod, lat_row, g1, w_in_l, seg, qg, kg, cos, sin, tm_lat)

        wg = [jnp.concatenate([_block_diag(lru_w_a[l, dr]), _block_diag(lru_w_x[l, dr])], axis=1).astype(BF16)
              for dr in range(2)]
        bg = [jnp.concatenate([lru_b_a[l, dr], lru_b_x[l, dr]]).reshape(1, 2 * d_lru) for dr in range(2)]
        lam = [lru_lam[l, dr].reshape(1, d_lru) for dr in range(2)]
        zero_state = jnp.zeros((bsz, 1, d_lru), F32)
        hf_c, last_f = _lru_pass(r_c, conv_w, conv_b, wg[0], bg[0], lam[0], zero_state, None, None, False, n_ctx)
        ya_c, last_b = _lru_pass(r_c, conv_w, conv_b, wg[1], bg[1], lam[1], zero_state, hf_c, gate_c, True, n_ctx)
        hf_l, _ = _lru_pass(r_l, conv_w, conv_b, wg[0], bg[0], lam[0], last_f, None, None, False, 512)
        ya_l, _ = _lru_pass(r_l, conv_w, conv_b, wg[1], bg[1], lam[1], last_b, hf_l, gate_l, True, 512)

        yb_l = _global_attention(gq_l, gk_l, gv_l, gk_c, gv_c, 512, 512)
        yc_l = _window_attention(sink, wq_l, wk_l, wv_l, wk_c, wv_c, 4)

        h1, a2 = _out_projection(h, ya_l, yb_l, yc_l, w_out_l, mod, lat_row, g2, tm_lat)
        final_g = None if need_ctx else final_norm_g.reshape(1, d)
        h = _conv_ffn(a2, h1, wu, fcw, fcb, wd, mod, lat_row, final_g, tm_lat)
        if need_ctx:
            yb_c, yc_c = _context_attention(sink, gq_c, gk_c, gv_c, wq_c, wk_c, wv_c)
            hc1, ac2 = _out_projection(hc, ya_c, yb_c, yc_c, w_out_l, mod, ctx_row, g2, tm_ctx)
            hc = _conv_ffn(ac2, hc1, wu, fcw, fcb, wd, mod, ctx_row, None, tm_ctx)
    return h
```

```python
import functools

import jax
import jax.numpy as jnp
from jax import lax
from jax.experimental import pallas as pl
from jax.experimental.pallas import tpu as pltpu

F32 = jnp.float32
BF16 = jnp.bfloat16

HEAD_DIM = 64
GRID_W = 64
WINDOW = 128
Q_BLOCK = 128
LRU_C = 8.0
ROPE_BASE = 10000.0
EPS = 1e-6
NEG_INF = -1e30
ATTN_SCALE = HEAD_DIM ** -0.5
LOG2E = 1.4426950408889634
Q_SCALE = ATTN_SCALE * LOG2E
LANES = 128
SUBLANES = 8
BF16_ROWS = 16
VMEM_LIMIT = 56 * 1024 * 1024


def _cparams(*sem):
    return pltpu.CompilerParams(dimension_semantics=sem, vmem_limit_bytes=VMEM_LIMIT)


def _rms_rows(x, g):
    ms = jnp.mean(x * x, axis=-1, keepdims=True)
    return x * lax.rsqrt(ms + EPS) * g


def _silu(x):
    return x * jax.nn.sigmoid(x)


def _gelu_tanh(x):
    c = 0.7978845608028654
    return 0.5 * x * (1.0 + jnp.tanh(c * (x + 0.044715 * (x * x * x))))


def _dot(a, b):
    return jnp.dot(a, b, preferred_element_type=F32)


def _dot_nt(a, b):
    return lax.dot_general(a, b, (((1,), (1,)), ((), ())), preferred_element_type=F32)


def _mod_kernel(c_ref, w_ref, b_ref, o_ref):
    s = _silu(c_ref[...])
    o_ref[0] = jnp.dot(s, w_ref[0], preferred_element_type=F32,
                       precision=lax.Precision.HIGHEST) + b_ref[0]


def _modulation(cvec, w_mod, b_mod):
    n_layers, d, d6 = w_mod.shape
    rows = cvec.shape[0]
    tn = 1536
    return pl.pallas_call(
        _mod_kernel,
        out_shape=jax.ShapeDtypeStruct((n_layers, rows, d6), F32),
        grid=(n_layers, d6 // tn),
        in_specs=[pl.BlockSpec((rows, d), lambda l, j: (0, 0)),
                  pl.BlockSpec((1, d, tn), lambda l, j: (l, 0, j)),
                  pl.BlockSpec((1, 1, tn), lambda l, j: (l, 0, j))],
        out_specs=pl.BlockSpec((1, rows, tn), lambda l, j: (l, 0, j)),
        compiler_params=_cparams("parallel", "parallel"),
        name="modulation",
    )(cvec, w_mod, b_mod.reshape(n_layers, 1, d6))


def _seg_mean_sq(x, seg):
    sq = x * x
    hi = sq.astype(BF16)
    lo = (sq - hi.astype(F32)).astype(BF16)
    return (_dot(hi, seg) + _dot(lo, seg)) * (1.0 / HEAD_DIM)


def _swap_halves(x):
    w = x.shape[-1]
    lane = lax.broadcasted_iota(jnp.int32, x.shape, 1)
    first = (lane % HEAD_DIM) < (HEAD_DIM // 2)
    up = pltpu.roll(x, w - HEAD_DIM // 2, axis=1)
    down = pltpu.roll(x, HEAD_DIM // 2, axis=1)
    return jnp.where(first, up, down)


def _dup_kv_heads(k):
    lane = lax.broadcasted_iota(jnp.int32, k.shape, 1)
    lo = lane < HEAD_DIM
    sw = pltpu.roll(k, HEAD_DIM, axis=1)
    return jnp.concatenate([jnp.where(lo, k, sw), jnp.where(lo, sw, k)], axis=1)


def _aug_kv_heads(v):
    lane = lax.broadcasted_iota(jnp.int32, v.shape, 1)
    lo = lane < HEAD_DIM
    sw = pltpu.roll(v, HEAD_DIM, axis=1)
    return jnp.concatenate([jnp.where(lo, v, 1.0), jnp.where(lo, sw, 1.0)], axis=1)


def _inproj_kernel(*refs, rope, d_lru):
    if rope:
        (h_ref, mod_ref, g_ref, w_ref, seg_ref, qg_ref, kg_ref, cos_ref, sin_ref,
         r_ref, gate_ref, gq_ref, gk_ref, gv_ref, wq_ref, wk_ref, wv_ref) = refs
    else:
        (h_ref, mod_ref, g_ref, w_ref, seg_ref, qg_ref, kg_ref,
         r_ref, gate_ref, gq_ref, gk_ref, gv_ref, wq_ref, wk_ref, wv_ref) = refs
    x = h_ref[0]
    a = _rms_rows(x, g_ref[...]) * (1.0 + mod_ref[0, 1:2, :]) + mod_ref[0, 0:1, :]
    y = _dot(a.astype(BF16), w_ref[...])
    o = 0
    r_ref[0] = y[:, o:o + d_lru]; o += d_lru
    gate_ref[0] = y[:, o:o + d_lru]; o += d_lru
    gq = y[:, o:o + 256]; o += 256
    gk = y[:, o:o + 128]; o += 128
    gv = y[:, o:o + 128]; o += 128
    wq = y[:, o:o + 256]; o += 256
    wk = y[:, o:o + 128]; o += 128
    wv = y[:, o:o + 128]

    seg = seg_ref[...]
    gq = gq * lax.rsqrt(_seg_mean_sq(gq, seg) + EPS) * qg_ref[...]
    gk = gk * lax.rsqrt(_seg_mean_sq(gk, seg[:128, :128]) + EPS) * kg_ref[...]
    if rope:
        cos = cos_ref[...]
        sin = sin_ref[...]
        gq = gq * cos + _swap_halves(gq) * sin
        wq = wq * cos + _swap_halves(wq) * sin
        gk = gk * cos[:, :128] + _swap_halves(gk) * sin[:, :128]
        wk = wk * cos[:, :128] + _swap_halves(wk) * sin[:, :128]
    gq_ref[0] = (gq * Q_SCALE).astype(BF16)
    wq_ref[0] = (wq * Q_SCALE).astype(BF16)
    gk_ref[0] = _dup_kv_heads(gk).astype(BF16)
    gv_ref[0] = _aug_kv_heads(gv).astype(BF16)
    wk_ref[0] = _dup_kv_heads(wk).astype(BF16)
    wv_ref[0] = _aug_kv_heads(wv).astype(BF16)


def _in_projection(h, mod, mod_row, g1, w_in, seg, qg, kg, cos, sin, tm):
    bsz, n, d = h.shape
    d_in = w_in.shape[1]
    d_lru = (d_in - 1024) // 2
    rope = cos is not None
    tok = lambda b, i: (b, i, 0)
    const2 = lambda b, i: (0, 0)
    in_specs = [pl.BlockSpec((1, tm, d), tok),
                pl.BlockSpec((1, 6, d), lambda b, i: (mod_row(b), 0, 0)),
                pl.BlockSpec((1, d), const2),
                pl.BlockSpec((d, d_in), const2),
                pl.BlockSpec((256, 256), const2),
                pl.BlockSpec((1, 256), const2),
                pl.BlockSpec((1, 128), const2)]
    args = [h, mod, g1, w_in, seg, qg, kg]
    if rope:
        in_specs += [pl.BlockSpec((tm, 256), lambda b, i: (i, 0)),
                     pl.BlockSpec((tm, 256), lambda b, i: (i, 0))]
        args += [cos, sin]
    out_shape = ([jax.ShapeDtypeStruct((bsz, n, d_lru), F32)] * 2
                 + [jax.ShapeDtypeStruct((bsz, n, 256), BF16)] * 6)
    out_specs = ([pl.BlockSpec((1, tm, d_lru), tok)] * 2
                 + [pl.BlockSpec((1, tm, 256), tok)] * 6)
    return pl.pallas_call(
        functools.partial(_inproj_kernel, rope=rope, d_lru=d_lru),
        out_shape=out_shape,
        grid=(bsz, n // tm),
        in_specs=in_specs,
        out_specs=out_specs,
        compiler_params=_cparams("parallel", "parallel"),
        name="in_projection_lat" if rope else "in_projection_ctx",
    )(*args)


def _sqrt_nonneg(x):
    return x * lax.rsqrt(jnp.maximum(x, 1e-30))


def _lru_kernel(*refs, reverse, mode, tl):
    if mode == "project":
        (rp_ref, rc_ref, rn_ref, cw_ref, cb_ref, wg_ref, bg_ref, lam_ref, h0_ref, ho_ref, gate_ref,
         h_ref, yb_ref, yc_ref, wa_ref, wb_ref, wc_ref, mod_ref, g2_ref,
         h1_ref, a2_ref, last_ref, a_scr, d_scr, carry_scr) = refs
    elif mode == "combine":
        (rp_ref, rc_ref, rn_ref, cw_ref, cb_ref, wg_ref, bg_ref, lam_ref, h0_ref, ho_ref, gate_ref,
         y_ref, last_ref, a_scr, d_scr, carry_scr) = refs
    else:
        (rp_ref, rc_ref, rn_ref, cw_ref, cb_ref, wg_ref, bg_ref, lam_ref, h0_ref,
         y_ref, last_ref, a_scr, d_scr, carry_scr) = refs
    j = pl.program_id(1)
    nt = pl.num_programs(1)
    t_idx = (nt - 1 - j) if reverse else j
    d_lru = rc_ref.shape[-1]

    @pl.when(j == 0)
    def _():
        carry_scr[...] = jnp.broadcast_to(h0_ref[0], carry_scr.shape)

    has_prev = (t_idx > 0).astype(F32)
    has_next = (t_idx < nt - 1).astype(F32)
    ext = jnp.concatenate([rp_ref[0] * has_prev, rc_ref[0], rn_ref[0] * has_next], axis=0)
    rows = tl + 2 * SUBLANES
    lo, hi = SUBLANES, SUBLANES + tl
    cw = cw_ref[...]
    u = cb_ref[...] + cw[2:3] * ext[lo:hi]
    u = u + cw[0:1] * pltpu.roll(ext, 2, axis=0)[lo:hi]
    u = u + cw[1:2] * pltpu.roll(ext, 1, axis=0)[lo:hi]
    u = u + cw[3:4] * pltpu.roll(ext, rows - 1, axis=0)[lo:hi]

    th = jnp.tanh(_dot(u.astype(BF16), wg_ref[...]) + bg_ref[...])
    nlam = -lam_ref[...]
    sp = jnp.maximum(nlam, 0.0) + jnp.log1p(jnp.exp(-jnp.abs(nlam)))
    half_c = (-0.5 * LRU_C * LOG2E) * sp
    a = jnp.exp2(half_c * th[:, :d_lru] + half_c)
    i_g = 0.5 * th[:, d_lru:] + 0.5
    a_scr[...] = a
    d_scr[...] = _sqrt_nonneg(1.0 - a * a) * i_g * u

    row = lax.broadcasted_iota(jnp.int32, (SUBLANES, d_lru), 0)
    n_chunks = tl // SUBLANES

    def chunk(c, carry):
        cc = (n_chunks - 1 - c) if reverse else c
        off = pl.multiple_of(cc * SUBLANES, SUBLANES)
        av = a_scr[pl.ds(off, SUBLANES), :]
        dv = d_scr[pl.ds(off, SUBLANES), :]
        for s in (1, 2, 4):
            if reverse:
                a_sh = pltpu.roll(av, SUBLANES - s, axis=0)
                d_sh = pltpu.roll(dv, SUBLANES - s, axis=0)
                ok = row < SUBLANES - s
            else:
                a_sh = pltpu.roll(av, s, axis=0)
                d_sh = pltpu.roll(dv, s, axis=0)
                ok = row >= s
            dv = jnp.where(ok, av * d_sh + dv, dv)
            av = jnp.where(ok, av * a_sh, av)
        hv = av * carry + dv
        d_scr[pl.ds(off, SUBLANES), :] = hv
        edge = hv[0:1, :] if reverse else hv[SUBLANES - 1:SUBLANES, :]
        return jnp.broadcast_to(edge, (SUBLANES, d_lru))

    carry = lax.fori_loop(0, n_chunks, chunk, carry_scr[...], unroll=2)
    carry_scr[...] = carry
    last_ref[0] = carry[0:1, :]

    hcur = d_scr[...]
    if mode == "state":
        y_ref[0] = hcur
        return
    ya = ((hcur + ho_ref[0]) * _gelu_tanh(gate_ref[0])).astype(BF16)
    if mode == "combine":
        y_ref[0] = ya
        return
    proj = _dot(ya, wa_ref[...]) + _dot(yb_ref[0], wb_ref[...]) + _dot(yc_ref[0], wc_ref[...])
    h1 = h_ref[0] + mod_ref[0, 2:3, :] * proj
    h1_ref[0] = h1
    a2 = _rms_rows(h1, g2_ref[...]) * (1.0 + mod_ref[0, 4:5, :]) + mod_ref[0, 3:4, :]
    a2_ref[0] = a2.astype(a2_ref.dtype)


def _lru_pass(r, conv_w, conv_b, wg, bg, lam, h0, reverse, tl, other=None, gate=None, proj=None):
    bsz, n, c = r.shape
    nt = n // tl
    hb = tl // SUBLANES
    nhb = n // SUBLANES
    mode = "state" if other is None else ("combine" if proj is None else "project")
    tt = (lambda j: nt - 1 - j) if reverse else (lambda j: j)
    cur = lambda b, j: (b, tt(j), 0)
    prev = lambda b, j: (b, jnp.maximum(tt(j) * hb - 1, 0), 0)
    nxt = lambda b, j: (b, jnp.minimum((tt(j) + 1) * hb, nhb - 1), 0)
    const2 = lambda b, j: (0, 0)
    state_spec = pl.BlockSpec((1, 1, c), lambda b, j: (b, 0, 0))
    in_specs = [pl.BlockSpec((1, SUBLANES, c), prev),
                pl.BlockSpec((1, tl, c), cur),
                pl.BlockSpec((1, SUBLANES, c), nxt),
                pl.BlockSpec(conv_w.shape, const2),
                pl.BlockSpec((1, c), const2),
                pl.BlockSpec(wg.shape, const2),
                pl.BlockSpec((1, 2 * c), const2),
                pl.BlockSpec((1, c), const2),
                state_spec]
    args = [r, r, r, conv_w, conv_b, wg, bg, lam, h0]
    if mode != "state":
        in_specs += [pl.BlockSpec((1, tl, c), cur), pl.BlockSpec((1, tl, c), cur)]
        args += [other, gate]
    if mode == "project":
        h, yb, yc, w_out, mod, mod_row, g2 = proj
        d = h.shape[-1]
        cb_, cc_ = yb.shape[-1], yc.shape[-1]
        in_specs += [pl.BlockSpec((1, tl, d), cur),
                     pl.BlockSpec((1, tl, cb_), cur), pl.BlockSpec((1, tl, cc_), cur),
                     pl.BlockSpec((c, d), const2), pl.BlockSpec((cb_, d), const2), pl.BlockSpec((cc_, d), const2),
                     pl.BlockSpec((1, 6, d), lambda b, j: (mod_row(b), 0, 0)),
                     pl.BlockSpec((1, d), const2)]
        args += [h, yb, yc, w_out[:c], w_out[c:c + cb_], w_out[c + cb_:], mod, g2]
        out_shape = [jax.ShapeDtypeStruct((bsz, n, d), F32), jax.ShapeDtypeStruct((bsz, n, d), BF16)]
        out_specs = [pl.BlockSpec((1, tl, d), cur), pl.BlockSpec((1, tl, d), cur)]
    else:
        out_shape = [jax.ShapeDtypeStruct((bsz, n, c), F32 if mode == "state" else BF16)]
        out_specs = [pl.BlockSpec((1, tl, c), cur)]
    return pl.pallas_call(
        functools.partial(_lru_kernel, reverse=reverse, mode=mode, tl=tl),
        out_shape=out_shape + [jax.ShapeDtypeStruct((bsz, 1, c), F32)],
        grid=(bsz, nt),
        in_specs=in_specs,
        out_specs=out_specs + [state_spec],
        scratch_shapes=[pltpu.VMEM((tl, c), F32), pltpu.VMEM((tl, c), F32),
                        pltpu.VMEM((SUBLANES, c), F32)],
        compiler_params=_cparams("parallel", "arbitrary"),
        name=("lru_bwd_" if reverse else "lru_fwd_") + mode,
    )(*args)


def _stack_heads(qg):
    lane = lax.broadcasted_iota(jnp.int32, qg.shape, 1)
    lo = lane < HEAD_DIM
    zero = jnp.zeros_like(qg)
    return jnp.concatenate([jnp.where(lo, qg, zero), jnp.where(lo, zero, qg)], axis=0)


def _normalise_heads(acc, extra_den, tq):
    den = pltpu.roll(acc, HEAD_DIM, axis=1)
    if extra_den is not None:
        den = den + extra_den
    res = acc / den
    lane = lax.broadcasted_iota(jnp.int32, (tq, LANES), 1)
    return jnp.where(lane < HEAD_DIM, res[:tq], pltpu.roll(res[tq:], HEAD_DIM, axis=1))


def _gattn_kernel(q_ref, kl_ref, vl_ref, kc_ref, vc_ref, o_ref, *, tk):
    tq = q_ref.shape[1]
    n_lat = kl_ref.shape[1]
    n_ctx = kc_ref.shape[1]
    chunks = [(kl_ref, vl_ref, c * tk, tk) for c in range(n_lat // tk)] + [(kc_ref, vc_ref, 0, n_ctx)]
    for g in range(2):
        gs = slice(LANES * g, LANES * (g + 1))
        q2 = _stack_heads(q_ref[0, :, gs])
        m = jnp.full((2 * tq, 1), NEG_INF, F32)
        acc = jnp.zeros((2 * tq, LANES), F32)
        for k_ref, v_ref, off, size in chunks:
            s = _dot_nt(q2, k_ref[0, off:off + size, gs])
            m_new = jnp.maximum(m, jnp.max(s, axis=1, keepdims=True))
            p = jnp.exp2(s - m_new)
            acc = jnp.exp2(m - m_new) * acc + _dot(p.astype(BF16), v_ref[0, off:off + size, gs])
            m = m_new
        o_ref[0, :, gs] = _normalise_heads(acc, None, tq).astype(o_ref.dtype)


def _global_attention(q, k_lat, v_lat, k_ctx, v_ctx, tq, tk):
    bsz, n, _ = q.shape
    n_ctx = k_ctx.shape[1]
    full = lambda b, i: (b, 0, 0)
    return pl.pallas_call(
        functools.partial(_gattn_kernel, tk=tk),
        out_shape=jax.ShapeDtypeStruct((bsz, n, 256), BF16),
        grid=(bsz, n // tq),
        in_specs=[pl.BlockSpec((1, tq, 256), lambda b, i: (b, i, 0)),
                  pl.BlockSpec((1, n, 256), full), pl.BlockSpec((1, n, 256), full),
                  pl.BlockSpec((1, n_ctx, 256), full), pl.BlockSpec((1, n_ctx, 256), full)],
        out_specs=pl.BlockSpec((1, tq, 256), lambda b, i: (b, i, 0)),
        compiler_params=_cparams("parallel", "arbitrary"),
        name="global_attention",
    )(q, k_lat, v_lat, k_ctx, v_ctx)


def _sink_column(sink_ref, g, rows):
    row = lax.broadcasted_iota(jnp.int32, (2 * rows, 1), 0)
    return jnp.where(row < rows, sink_ref[2 * g], sink_ref[2 * g + 1]) * LOG2E


def _wattn_kernel(sink_ref, q_ref, kp_ref, kc_ref, kn_ref, vp_ref, vc_ref, vn_ref, kx_ref, vx_ref, o_ref,
                  kcat, vcat, *, nsub):
    i = pl.program_id(1)
    nb = pl.num_programs(1)
    w = WINDOW
    tq = nsub * w
    kcat[0:w] = kp_ref[0]
    kcat[w:w + tq] = kc_ref[0]
    kcat[w + tq:] = kn_ref[0]
    vcat[0:w] = vp_ref[0]
    vcat[w:w + tq] = vc_ref[0]
    vcat[w + tq:] = vn_ref[0]
    pw = 2 * w
    npair = nsub // 2
    rq = lax.broadcasted_iota(jnp.int32, (2 * pw, 2 * pw), 0) % pw
    col = lax.broadcasted_iota(jnp.int32, (2 * pw, 2 * pw), 1)
    band = (col >= rq) & (col <= rq + 2 * w)
    for g in range(2):
        gs = slice(LANES * g, LANES * (g + 1))
        q2_all = _stack_heads(q_ref[0, :, gs])
        s_ctx_all = _dot_nt(q2_all, kx_ref[0, :, gs])
        for p in range(npair):
            valid = band
            if p == 0:
                valid = valid & ((col >= w) | (i > 0))
            if p == npair - 1:
                valid = valid & ((col < 3 * w) | (i < nb - 1))
            r0 = slice(p * pw, (p + 1) * pw)
            r1 = slice(tq + p * pw, tq + (p + 1) * pw)
            keys = slice(p * pw, p * pw + 2 * pw)
            q2 = jnp.concatenate([q2_all[r0], q2_all[r1]], axis=0)
            s_loc = jnp.where(valid, _dot_nt(q2, kcat[keys, gs]), NEG_INF)
            s_ctx = jnp.concatenate([s_ctx_all[r0], s_ctx_all[r1]], axis=0)
            s_sink = _sink_column(sink_ref, g, pw)
            m = jnp.maximum(jnp.maximum(jnp.max(s_loc, axis=1, keepdims=True),
                                        jnp.max(s_ctx, axis=1, keepdims=True)), s_sink)
            p_loc = jnp.exp2(s_loc - m)
            p_ctx = jnp.exp2(s_ctx - m)
            acc = _dot(p_loc.astype(BF16), vcat[keys, gs]) + _dot(p_ctx.astype(BF16), vx_ref[0, :, gs])
            o_ref[0, r0, gs] = _normalise_heads(acc, jnp.exp2(s_sink - m), pw).astype(o_ref.dtype)


def _window_attention(sink, q, k, v, k_ctx, v_ctx, nsub):
    bsz, n, _ = q.shape
    n_ctx = k_ctx.shape[1]
    tq = nsub * Q_BLOCK
    nb128 = n // Q_BLOCK
    cur = lambda b, i: (b, i, 0)
    prev = lambda b, i: (b, jnp.maximum(i * nsub - 1, 0), 0)
    nxt = lambda b, i: (b, jnp.minimum((i + 1) * nsub, nb128 - 1), 0)
    full = lambda b, i: (b, 0, 0)
    halo = lambda im: pl.BlockSpec((1, Q_BLOCK, 256), im)
    big = pl.BlockSpec((1, tq, 256), cur)
    return pl.pallas_call(
        functools.partial(_wattn_kernel, nsub=nsub),
        out_shape=jax.ShapeDtypeStruct((bsz, n, 256), BF16),
        grid=(bsz, n // tq),
        in_specs=[pl.BlockSpec(memory_space=pltpu.SMEM),
                  big, halo(prev), big, halo(nxt), halo(prev), big, halo(nxt),
                  pl.BlockSpec((1, n_ctx, 256), full), pl.BlockSpec((1, n_ctx, 256), full)],
        out_specs=big,
        scratch_shapes=[pltpu.VMEM((tq + 2 * Q_BLOCK, 256), BF16), pltpu.VMEM((tq + 2 * Q_BLOCK, 256), BF16)],
        compiler_params=_cparams("parallel", "arbitrary"),
        name="window_attention",
    )(sink, q, k, k, k, v, v, v, k_ctx, v_ctx)


def _cattn_kernel(sink_ref, gq_ref, gk_ref, gv_ref, wq_ref, wk_ref, wv_ref, yb_ref, yc_ref):
    tq = gq_ref.shape[1]
    for g in range(2):
        gs = slice(LANES * g, LANES * (g + 1))
        s = _dot_nt(_stack_heads(gq_ref[0, :, gs]), gk_ref[0, :, gs])
        p = jnp.exp2(s - jnp.max(s, axis=1, keepdims=True))
        acc = _dot(p.astype(BF16), gv_ref[0, :, gs])
        yb_ref[0, :, gs] = _normalise_heads(acc, None, tq).astype(yb_ref.dtype)
        s = _dot_nt(_stack_heads(wq_ref[0, :, gs]), wk_ref[0, :, gs])
        s_sink = _sink_column(sink_ref, g, tq)
        m = jnp.maximum(jnp.max(s, axis=1, keepdims=True), s_sink)
        p = jnp.exp2(s - m)
        acc = _dot(p.astype(BF16), wv_ref[0, :, gs])
        yc_ref[0, :, gs] = _normalise_heads(acc, jnp.exp2(s_sink - m), tq).astype(yc_ref.dtype)


def _context_attention(sink, gq, gk, gv, wq, wk, wv):
    bsz, n_ctx, _ = gq.shape
    blk = pl.BlockSpec((1, n_ctx, 256), lambda b: (b, 0, 0))
    return pl.pallas_call(
        _cattn_kernel,
        out_shape=[jax.ShapeDtypeStruct((bsz, n_ctx, 256), BF16)] * 2,
        grid=(bsz,),
        in_specs=[pl.BlockSpec(memory_space=pltpu.SMEM)] + [blk] * 6,
        out_specs=[blk, blk],
        compiler_params=_cparams("parallel"),
        name="context_attention",
    )(sink, gq, gk, gv, wq, wk, wv)


def _ffn_kernel(*refs, tm, ck, final):
    if final:
        (ap_ref, ac_ref, an_ref, h1_ref, wu_ref, cw_ref, cb_ref, wd_ref, mod_ref, gf_ref,
         o_ref, ext_scr, up_scr, act_scr) = refs
    else:
        (ap_ref, ac_ref, an_ref, h1_ref, wu_ref, cw_ref, cb_ref, wd_ref, mod_ref,
         o_ref, ext_scr, up_scr, act_scr) = refs
    i = pl.program_id(1)
    nt = pl.num_programs(1)
    h0 = BF16_ROWS
    ext_scr[0:h0] = jnp.where(i > 0, ap_ref[0], jnp.zeros_like(ap_ref[0]))
    ext_scr[h0:h0 + tm] = ac_ref[0]
    ext_scr[h0 + tm:] = jnp.where(i < nt - 1, an_ref[0], jnp.zeros_like(an_ref[0]))
    n_chunks = wu_ref.shape[0]

    def up_proj(c, slot):
        up_scr[slot] = _dot(ext_scr[...], wu_ref[c])

    def conv_act(c, slot):
        up = up_scr[slot]
        cw = cw_ref[c]
        u = cb_ref[c] + cw[0:1] * up[h0 - 1:h0 - 1 + tm] + cw[1:2] * up[h0:h0 + tm] + cw[2:3] * up[h0 + 1:h0 + 1 + tm]
        act_scr[c] = (_silu(u[:, :ck]) * u[:, ck:]).astype(BF16)

    up_proj(0, 0)
    n_pairs = (n_chunks - 1) // 2

    def pair(j, _):
        c = 2 * j
        up_proj(c + 1, 1)
        conv_act(c, 0)
        up_proj(c + 2, 0)
        conv_act(c + 1, 1)
        return 0

    for j in range(n_pairs):
        pair(j, 0)
    c = 2 * n_pairs
    if c + 1 < n_chunks:
        up_proj(c + 1, 1)
    conv_act(c, 0)
    if c + 1 < n_chunks:
        conv_act(c + 1, 1)

    acc = _dot(act_scr[0], wd_ref[0])
    for c in range(1, n_chunks):
        acc = acc + _dot(act_scr[c], wd_ref[c])
    h2 = h1_ref[0] + mod_ref[0, 5:6, :] * acc
    if final:
        h2 = _rms_rows(h2, gf_ref[...])
    o_ref[0] = h2


def _conv_ffn(a2, h1, wu, cw, cb, wd, mod, mod_row, final_g, tm):
    bsz, n, d = h1.shape
    n_chunks, _, ck2 = wu.shape
    ck = ck2 // 2
    hb = tm // BF16_ROWS
    nhb = n // BF16_ROWS
    final = final_g is not None
    tok = lambda b, i: (b, i, 0)
    prev = lambda b, i: (b, jnp.maximum(i * hb - 1, 0), 0)
    nxt = lambda b, i: (b, jnp.minimum((i + 1) * hb, nhb - 1), 0)
    const3 = lambda b, i: (0, 0, 0)
    single = pl.Buffered(1)
    in_specs = [pl.BlockSpec((1, BF16_ROWS, d), prev),
                pl.BlockSpec((1, tm, d), tok),
                pl.BlockSpec((1, BF16_ROWS, d), nxt),
                pl.BlockSpec((1, tm, d), tok),
                pl.BlockSpec(wu.shape, const3, pipeline_mode=single),
                pl.BlockSpec(cw.shape, const3),
                pl.BlockSpec(cb.shape, const3),
                pl.BlockSpec(wd.shape, const3, pipeline_mode=single),
                pl.BlockSpec((1, 6, d), lambda b, i: (mod_row(b), 0, 0))]
    args = [a2, a2, a2, h1, wu, cw, cb, wd, mod]
    if final:
        in_specs.append(pl.BlockSpec((1, d), lambda b, i: (0, 0)))
        args.append(final_g)
    return pl.pallas_call(
        functools.partial(_ffn_kernel, tm=tm, ck=ck, final=final),
        out_shape=jax.ShapeDtypeStruct((bsz, n, d), F32),
        grid=(bsz, n // tm),
        in_specs=in_specs,
        out_specs=pl.BlockSpec((1, tm, d), tok),
        scratch_shapes=[pltpu.VMEM((tm + 2 * BF16_ROWS, d), BF16),
                        pltpu.VMEM((2, tm + 2 * BF16_ROWS, ck2), F32),
                        pltpu.VMEM((n_chunks, tm, ck), BF16)],
        compiler_params=_cparams("parallel", "parallel"),
        name="conv_ffn_final" if final else "conv_ffn",
    )(*args)


def _rope_tables(n):
    pos = jnp.arange(n)
    row = (pos // GRID_W).astype(F32)
    col = (pos % GRID_W).astype(F32)
    n_freq = HEAD_DIM // 4
    inv = ROPE_BASE ** (-jnp.arange(n_freq, dtype=F32) / n_freq)
    ang = jnp.concatenate([row[:, None] * inv, col[:, None] * inv], axis=-1)
    cos, sin = jnp.cos(ang), jnp.sin(ang)
    cos_h = jnp.concatenate([cos, cos], axis=-1)
    sin_h = jnp.concatenate([-sin, sin], axis=-1)
    return jnp.tile(cos_h, (1, 4)), jnp.tile(sin_h, (1, 4))


def _block_diag(w):
    nb, c, d = w.shape
    eye = jnp.eye(nb, dtype=w.dtype)
    return jnp.einsum("ncd,nm->ncmd", w, eye).reshape(nb * c, nb * d)


def _ffn_chunks(d_ff):
    for ck in (256, 128):
        if d_ff % ck == 0:
            return ck
    raise ValueError("d_ff must be a multiple of 128")


def kernel(x, c, ctx, c_ctx, w_mod, b_mod, norm1_g, w_in, lru_conv_w, lru_conv_b, lru_w_a, lru_b_a, lru_w_x, lru_b_x, lru_lam, ga_q_norm_g, ga_k_norm_g, wa_sink, w_out, norm2_g, w_up, ffn_conv_w, ffn_conv_b, w_down, final_norm_g):
    bsz, n, d = x.shape
    n_ctx = ctx.shape[1]
    depth = w_in.shape[0]
    d_lru = lru_conv_w.shape[-1]
    d_ff = w_down.shape[1]
    assert w_in.shape[2] == 2 * d_lru + 1024 and n % 512 == 0 and n_ctx % 128 == 0 and bsz <= 16

    rows = 24
    cvec = jnp.zeros((rows, d), F32).at[:bsz].set(c).at[bsz].set(c_ctx)
    mod_all = _modulation(cvec, w_mod, b_mod).reshape(depth, rows, 6, d)
    lat_row = lambda b: b
    ctx_row = lambda b: bsz

    cos, sin = _rope_tables(n)
    seg = _block_diag(jnp.ones((4, HEAD_DIM, HEAD_DIM), BF16))
    ck = _ffn_chunks(d_ff)
    n_ck = d_ff // ck

    tm_lat, tm_ctx = 512, n_ctx
    h, hc = x, ctx
    for l in range(depth):
        need_ctx = l < depth - 1
        mod = mod_all[l]
        g1 = norm1_g[l].reshape(1, d)
        g2 = norm2_g[l].reshape(1, d)
        w_in_l = w_in[l].astype(BF16)
        qg = jnp.tile(ga_q_norm_g[l], 4).reshape(1, 256)
        kg = jnp.tile(ga_k_norm_g[l], 2).reshape(1, 128)
        sink = wa_sink[l]
        conv_w = lru_conv_w[l]
        conv_b = lru_conv_b[l].reshape(1, d_lru)
        w_out_l = w_out[l].astype(BF16)
        wu = w_up[l].astype(BF16)
        wu = jnp.concatenate([wu[:, :d_ff].reshape(d, n_ck, ck), wu[:, d_ff:].reshape(d, n_ck, ck)], axis=-1)
        wu = wu.transpose(1, 0, 2)
        fcw = ffn_conv_w[l]
        fcw = jnp.concatenate([fcw[:, :d_ff].reshape(-1, n_ck, ck), fcw[:, d_ff:].reshape(-1, n_ck, ck)], axis=-1)
        fcw = fcw.transpose(1, 0, 2)
        fcb = ffn_conv_b[l]
        fcb = jnp.concatenate([fcb[:d_ff].reshape(n_ck, 1, ck), fcb[d_ff:].reshape(n_ck, 1, ck)], axis=-1)
        wd = w_down[l].astype(BF16).reshape(n_ck, ck, d)

        r_c, gate_c, gq_c, gk_c, gv_c, wq_c, wk_c, wv_c = _in_projection(
            hc, mod, ctx_row, g1, w_in_l, seg, qg, kg, None, None, tm_ctx)
        r_l, gate_l, gq_l, gk_l, gv_l, wq_l, wk_l, wv_l = _in_projection(
            h, mod, lat_row, g1, w_in_l, seg, qg, kg, cos, sin, tm_lat)

        wg = [(0.5 * jnp.concatenate([_block_diag(lru_w_a[l, dr]), _block_diag(lru_w_x[l, dr])], axis=1)).astype(BF16)
              for dr in range(2)]
        bg = [0.5 * jnp.concatenate([lru_b_a[l, dr], lru_b_x[l, dr]]).reshape(1, 2 * d_lru) for dr in range(2)]
        lam = [lru_lam[l, dr].reshape(1, d_lru) for dr in range(2)]
        zero_state = jnp.zeros((bsz, 1, d_lru), F32)
        lru = lambda r, dr, h0, rev, tl, **kw: _lru_pass(r, conv_w, conv_b, wg[dr], bg[dr], lam[dr], h0, rev, tl, **kw)
        hf_c, last_f = lru(r_c, 0, zero_state, False, n_ctx)
        hf_l, _ = lru(r_l, 0, last_f, False, 512)

        yb_l = _global_attention(gq_l, gk_l, gv_l, gk_c, gv_c, 512, 512)
        yc_l = _window_attention(sink, wq_l, wk_l, wv_l, wk_c, wv_c, 4)

        if need_ctx:
            yb_c, yc_c = _context_attention(sink, gq_c, gk_c, gv_c, wq_c, wk_c, wv_c)
            hc1, ac2, last_b = lru(r_c, 1, zero_state, True, n_ctx, other=hf_c, gate=gate_c,
                                   proj=(hc, yb_c, yc_c, w_out_l, mod, ctx_row, g2))
        else:
            _, last_b = lru(r_c, 1, zero_state, True, n_ctx)
        h1, a2, _ = lru(r_l, 1, last_b, True, 512, other=hf_l, gate=gate_l,
                        proj=(h, yb_l, yc_l, w_out_l, mod, lat_row, g2))

        final_g = None if need_ctx else final_norm_g.reshape(1, d)
        h = _conv_ffn(a2, h1, wu, fcw, fcb, wd, mod, lat_row, final_g, tm_lat)
        if need_ctx:
            hc = _conv_ffn(ac2, hc1, wu, fcw, fcb, wd, mod, ctx_row, None, tm_ctx)
    return h
```

```python
import functools

import jax
import jax.numpy as jnp
from jax import lax
from jax.experimental import pallas as pl
from jax.experimental.pallas import tpu as pltpu

F32 = jnp.float32
BF16 = jnp.bfloat16

HEAD_DIM = 64
GRID_W = 64
WINDOW = 128
Q_BLOCK = 128
LRU_C = 8.0
ROPE_BASE = 10000.0
EPS = 1e-6
NEG_INF = -1e30
ATTN_SCALE = HEAD_DIM ** -0.5
LOG2E = 1.4426950408889634
Q_SCALE = ATTN_SCALE * LOG2E
LANES = 128
SUBLANES = 8
BF16_ROWS = 16
VMEM_LIMIT = 56 * 1024 * 1024


def _cparams(*sem):
    return pltpu.CompilerParams(dimension_semantics=sem, vmem_limit_bytes=VMEM_LIMIT)


def _rms_rows(x, g):
    ms = jnp.mean(x * x, axis=-1, keepdims=True)
    return x * lax.rsqrt(ms + EPS) * g


def _silu(x):
    return x * jax.nn.sigmoid(x)


def _gelu_tanh(x):
    c = 0.7978845608028654
    return 0.5 * x * (1.0 + jnp.tanh(c * (x + 0.044715 * (x * x * x))))


def _dot(a, b):
    return jnp.dot(a, b, preferred_element_type=F32)


def _dot_nt(a, b):
    return lax.dot_general(a, b, (((1,), (1,)), ((), ())), preferred_element_type=F32)


def _mod_kernel(c_ref, w_ref, b_ref, o_ref):
    s = _silu(c_ref[...])
    o_ref[0] = jnp.dot(s, w_ref[0], preferred_element_type=F32,
                       precision=lax.Precision.HIGHEST) + b_ref[0]


def _modulation(cvec, w_mod, b_mod):
    n_layers, d, d6 = w_mod.shape
    rows = cvec.shape[0]
    tn = 1536
    return pl.pallas_call(
        _mod_kernel,
        out_shape=jax.ShapeDtypeStruct((n_layers, rows, d6), F32),
        grid=(n_layers, d6 // tn),
        in_specs=[pl.BlockSpec((rows, d), lambda l, j: (0, 0)),
                  pl.BlockSpec((1, d, tn), lambda l, j: (l, 0, j)),
                  pl.BlockSpec((1, 1, tn), lambda l, j: (l, 0, j))],
        out_specs=pl.BlockSpec((1, rows, tn), lambda l, j: (l, 0, j)),
        compiler_params=_cparams("parallel", "parallel"),
        name="modulation",
    )(cvec, w_mod, b_mod.reshape(n_layers, 1, d6))


def _seg_mean_sq(x, seg):
    sq = x * x
    hi = sq.astype(BF16)
    lo = (sq - hi.astype(F32)).astype(BF16)
    return (_dot(hi, seg) + _dot(lo, seg)) * (1.0 / HEAD_DIM)


def _swap_halves(x):
    w = x.shape[-1]
    lane = lax.broadcasted_iota(jnp.int32, x.shape, 1)
    first = (lane % HEAD_DIM) < (HEAD_DIM // 2)
    up = pltpu.roll(x, w - HEAD_DIM // 2, axis=1)
    down = pltpu.roll(x, HEAD_DIM // 2, axis=1)
    return jnp.where(first, up, down)


def _dup_kv_heads(k):
    lane = lax.broadcasted_iota(jnp.int32, k.shape, 1)
    lo = lane < HEAD_DIM
    sw = pltpu.roll(k, HEAD_DIM, axis=1)
    return jnp.concatenate([jnp.where(lo, k, sw), jnp.where(lo, sw, k)], axis=1)


def _aug_kv_heads(v):
    lane = lax.broadcasted_iota(jnp.int32, v.shape, 1)
    lo = lane < HEAD_DIM
    sw = pltpu.roll(v, HEAD_DIM, axis=1)
    return jnp.concatenate([jnp.where(lo, v, 1.0), jnp.where(lo, sw, 1.0)], axis=1)


def _sqrt_nonneg(x):
    return x * lax.rsqrt(jnp.maximum(x, 1e-30))


def _lru_gates(u, wg_ref, bg_ref, lam_ref, a_scr, d_scr):
    d_lru = u.shape[-1]
    th = jnp.tanh(_dot(u.astype(BF16), wg_ref[...]) + bg_ref[...])
    nlam = -lam_ref[...]
    sp = jnp.maximum(nlam, 0.0) + jnp.log1p(jnp.exp(-jnp.abs(nlam)))
    half_c = (-0.5 * LRU_C * LOG2E) * sp
    a = jnp.exp2(half_c * th[:, :d_lru] + half_c)
    i_g = 0.5 * th[:, d_lru:] + 0.5
    a_scr[...] = a
    d_scr[...] = _sqrt_nonneg(1.0 - a * a) * i_g * u


def _lru_scan(a_scr, d_scr, carry_scr, reverse, unroll=2):
    tl, d_lru = a_scr.shape
    row = lax.broadcasted_iota(jnp.int32, (SUBLANES, d_lru), 0)
    n_chunks = tl // SUBLANES

    def chunk(c, carry):
        cc = (n_chunks - 1 - c) if reverse else c
        off = pl.multiple_of(cc * SUBLANES, SUBLANES)
        av = a_scr[pl.ds(off, SUBLANES), :]
        dv = d_scr[pl.ds(off, SUBLANES), :]
        for s in (1, 2, 4):
            if reverse:
                a_sh = pltpu.roll(av, SUBLANES - s, axis=0)
                d_sh = pltpu.roll(dv, SUBLANES - s, axis=0)
                ok = row < SUBLANES - s
            else:
                a_sh = pltpu.roll(av, s, axis=0)
                d_sh = pltpu.roll(dv, s, axis=0)
                ok = row >= s
            dv = jnp.where(ok, av * d_sh + dv, dv)
            av = jnp.where(ok, av * a_sh, av)
        hv = av * carry + dv
        d_scr[pl.ds(off, SUBLANES), :] = hv
        edge = hv[0:1, :] if reverse else hv[SUBLANES - 1:SUBLANES, :]
        return jnp.broadcast_to(edge, (SUBLANES, d_lru))

    carry = lax.fori_loop(0, n_chunks, chunk, carry_scr[...], unroll=unroll)
    carry_scr[...] = carry
    return carry


def _inproj_kernel(*refs, rope, d_lru, tm):
    if rope:
        (h_ref, hn_ref, mod_ref, g_ref, w_ref, seg_ref, qg_ref, kg_ref, cos_ref, sin_ref,
         cw_ref, cb_ref, wg_ref, bg_ref, lam_ref, h0_ref,
         u_ref, gate_ref, hf_ref, last_ref, gq_ref, gk_ref, gv_ref, wq_ref, wk_ref, wv_ref,
         tail_scr, a_scr, d_scr, carry_scr) = refs
    else:
        (h_ref, hn_ref, mod_ref, g_ref, w_ref, seg_ref, qg_ref, kg_ref,
         cw_ref, cb_ref, wg_ref, bg_ref, lam_ref, h0_ref,
         u_ref, gate_ref, hf_ref, last_ref, gq_ref, gk_ref, gv_ref, wq_ref, wk_ref, wv_ref,
         tail_scr, a_scr, d_scr, carry_scr) = refs
    i = pl.program_id(1)
    nt = pl.num_programs(1)

    @pl.when(i == 0)
    def _():
        carry_scr[...] = jnp.broadcast_to(h0_ref[0], carry_scr.shape)
        tail_scr[...] = jnp.zeros_like(tail_scr)

    x = jnp.concatenate([h_ref[0], hn_ref[0]], axis=0)
    a = _rms_rows(x, g_ref[...]) * (1.0 + mod_ref[0, 1:2, :]) + mod_ref[0, 0:1, :]
    y_ext = _dot(a.astype(BF16), w_ref[...])
    y = y_ext[:tm]
    o = 0
    r = y[:, o:o + d_lru]; o += d_lru
    gate_ref[0] = y[:, o:o + d_lru]; o += d_lru
    gq = y[:, o:o + 256]; o += 256
    gk = y[:, o:o + 128]; o += 128
    gv = y[:, o:o + 128]; o += 128
    wq = y[:, o:o + 256]; o += 256
    wk = y[:, o:o + 128]; o += 128
    wv = y[:, o:o + 128]

    seg = seg_ref[...]
    gq = gq * lax.rsqrt(_seg_mean_sq(gq, seg) + EPS) * qg_ref[...]
    gk = gk * lax.rsqrt(_seg_mean_sq(gk, seg[:128, :128]) + EPS) * kg_ref[...]
    if rope:
        cos = cos_ref[...]
        sin = sin_ref[...]
        gq = gq * cos + _swap_halves(gq) * sin
        wq = wq * cos + _swap_halves(wq) * sin
        gk = gk * cos[:, :128] + _swap_halves(gk) * sin[:, :128]
        wk = wk * cos[:, :128] + _swap_halves(wk) * sin[:, :128]
    gq_ref[0] = (gq * Q_SCALE).astype(BF16)
    wq_ref[0] = (wq * Q_SCALE).astype(BF16)
    gk_ref[0] = _dup_kv_heads(gk).astype(BF16)
    gv_ref[0] = _aug_kv_heads(gv).astype(BF16)
    wk_ref[0] = _dup_kv_heads(wk).astype(BF16)
    wv_ref[0] = _aug_kv_heads(wv).astype(BF16)

    has_next = (i < nt - 1).astype(F32)
    ext = jnp.concatenate([tail_scr[...], r, y_ext[tm:, :d_lru] * has_next], axis=0)
    tail_scr[...] = r[tm - SUBLANES:]
    rows = tm + 2 * SUBLANES
    lo, hi = SUBLANES, SUBLANES + tm
    cw = cw_ref[...]
    u = cb_ref[...] + cw[2:3] * ext[lo:hi]
    u = u + cw[0:1] * pltpu.roll(ext, 2, axis=0)[lo:hi]
    u = u + cw[1:2] * pltpu.roll(ext, 1, axis=0)[lo:hi]
    u = u + cw[3:4] * pltpu.roll(ext, rows - 1, axis=0)[lo:hi]
    u_ref[0] = u

    _lru_gates(u, wg_ref, bg_ref, lam_ref, a_scr, d_scr)
    carry = _lru_scan(a_scr, d_scr, carry_scr, False, unroll=True)
    hf_ref[0] = d_scr[...]
    last_ref[0] = carry[0:1, :]


def _in_projection(h, mod, mod_row, g1, w_in, seg, qg, kg, cos, sin, conv_w, conv_b, wg, bg, lam, h0, tm):
    bsz, n, d = h.shape
    d_in = w_in.shape[1]
    d_lru = conv_w.shape[-1]
    rope = cos is not None
    nhb = n // SUBLANES
    tok = lambda b, i: (b, i, 0)
    nxt = lambda b, i: (b, jnp.minimum((i + 1) * (tm // SUBLANES), nhb - 1), 0)
    const2 = lambda b, i: (0, 0)
    state_spec = pl.BlockSpec((1, 1, d_lru), lambda b, i: (b, 0, 0))
    in_specs = [pl.BlockSpec((1, tm, d), tok),
                pl.BlockSpec((1, SUBLANES, d), nxt),
                pl.BlockSpec((1, 6, d), lambda b, i: (mod_row(b), 0, 0)),
                pl.BlockSpec((1, d), const2),
                pl.BlockSpec((d, d_in), const2),
                pl.BlockSpec((256, 256), const2),
                pl.BlockSpec((1, 256), const2),
                pl.BlockSpec((1, 128), const2)]
    args = [h, h, mod, g1, w_in, seg, qg, kg]
    if rope:
        in_specs += [pl.BlockSpec((tm, 256), lambda b, i: (i, 0)),
                     pl.BlockSpec((tm, 256), lambda b, i: (i, 0))]
        args += [cos, sin]
    in_specs += [pl.BlockSpec(conv_w.shape, const2),
                 pl.BlockSpec((1, d_lru), const2),
                 pl.BlockSpec(wg.shape, const2),
                 pl.BlockSpec((1, 2 * d_lru), const2),
                 pl.BlockSpec((1, d_lru), const2),
                 state_spec]
    args += [conv_w, conv_b, wg, bg, lam, h0]
    out_shape = ([jax.ShapeDtypeStruct((bsz, n, d_lru), F32)] * 3
                 + [jax.ShapeDtypeStruct((bsz, 1, d_lru), F32)]
                 + [jax.ShapeDtypeStruct((bsz, n, 256), BF16)] * 6)
    out_specs = ([pl.BlockSpec((1, tm, d_lru), tok)] * 3 + [state_spec]
                 + [pl.BlockSpec((1, tm, 256), tok)] * 6)
    return pl.pallas_call(
        functools.partial(_inproj_kernel, rope=rope, d_lru=d_lru, tm=tm),
        out_shape=out_shape,
        grid=(bsz, n // tm),
        in_specs=in_specs,
        out_specs=out_specs,
        scratch_shapes=[pltpu.VMEM((SUBLANES, d_lru), F32), pltpu.VMEM((tm, d_lru), F32),
                        pltpu.VMEM((tm, d_lru), F32), pltpu.VMEM((SUBLANES, d_lru), F32)],
        compiler_params=_cparams("parallel", "arbitrary"),
        name="in_projection_lat" if rope else "in_projection_ctx",
    )(*args)


def _lru_bwd_kernel(*refs, project):
    if project:
        (u_ref, wg_ref, bg_ref, lam_ref, h0_ref, hf_ref, gate_ref,
         h_ref, yb_ref, yc_ref, wa_ref, wb_ref, wc_ref, mod_ref, g2_ref,
         h1_ref, a2_ref, last_ref, a_scr, d_scr, carry_scr) = refs
    else:
        (u_ref, wg_ref, bg_ref, lam_ref, h0_ref, last_ref, a_scr, d_scr, carry_scr) = refs
    j = pl.program_id(1)

    @pl.when(j == 0)
    def _():
        carry_scr[...] = jnp.broadcast_to(h0_ref[0], carry_scr.shape)

    _lru_gates(u_ref[0], wg_ref, bg_ref, lam_ref, a_scr, d_scr)
    carry = _lru_scan(a_scr, d_scr, carry_scr, True, unroll=True)
    last_ref[0] = carry[0:1, :]
    if not project:
        return
    ya = ((d_scr[...] + hf_ref[0]) * _gelu_tanh(gate_ref[0])).astype(BF16)
    proj = _dot(ya, wa_ref[...]) + _dot(yb_ref[0], wb_ref[...]) + _dot(yc_ref[0], wc_ref[...])
    h1 = h_ref[0] + mod_ref[0, 2:3, :] * proj
    h1_ref[0] = h1
    a2 = _rms_rows(h1, g2_ref[...]) * (1.0 + mod_ref[0, 4:5, :]) + mod_ref[0, 3:4, :]
    a2_ref[0] = a2.astype(a2_ref.dtype)


def _lru_backward(u, wg, bg, lam, h0, tl, proj=None):
    bsz, n, c = u.shape
    nt = n // tl
    cur = lambda b, j: (b, nt - 1 - j, 0)
    const2 = lambda b, j: (0, 0)
    state_spec = pl.BlockSpec((1, 1, c), lambda b, j: (b, 0, 0))
    in_specs = [pl.BlockSpec((1, tl, c), cur),
                pl.BlockSpec(wg.shape, const2),
                pl.BlockSpec((1, 2 * c), const2),
                pl.BlockSpec((1, c), const2),
                state_spec]
    args = [u, wg, bg, lam, h0]
    out_shape, out_specs = [], []
    if proj is not None:
        hf, gate, h, yb, yc, w_out, mod, mod_row, g2 = proj
        d = h.shape[-1]
        cb_, cc_ = yb.shape[-1], yc.shape[-1]
        in_specs += [pl.BlockSpec((1, tl, c), cur), pl.BlockSpec((1, tl, c), cur),
                     pl.BlockSpec((1, tl, d), cur),
                     pl.BlockSpec((1, tl, cb_), cur), pl.BlockSpec((1, tl, cc_), cur),
                     pl.BlockSpec((c, d), const2), pl.BlockSpec((cb_, d), const2), pl.BlockSpec((cc_, d), const2),
                     pl.BlockSpec((1, 6, d), lambda b, j: (mod_row(b), 0, 0)),
                     pl.BlockSpec((1, d), const2)]
        args += [hf, gate, h, yb, yc, w_out[:c], w_out[c:c + cb_], w_out[c + cb_:], mod, g2]
        out_shape = [jax.ShapeDtypeStruct((bsz, n, d), F32), jax.ShapeDtypeStruct((bsz, n, d), BF16)]
        out_specs = [pl.BlockSpec((1, tl, d), cur), pl.BlockSpec((1, tl, d), cur)]
    return pl.pallas_call(
        functools.partial(_lru_bwd_kernel, project=proj is not None),
        out_shape=out_shape + [jax.ShapeDtypeStruct((bsz, 1, c), F32)],
        grid=(bsz, nt),
        in_specs=in_specs,
        out_specs=out_specs + [state_spec],
        scratch_shapes=[pltpu.VMEM((tl, c), F32), pltpu.VMEM((tl, c), F32),
                        pltpu.VMEM((SUBLANES, c), F32)],
        compiler_params=_cparams("parallel", "arbitrary"),
        name="lru_bwd_project" if proj is not None else "lru_bwd_state",
    )(*args)


def _stack_heads(qg):
    lane = lax.broadcasted_iota(jnp.int32, qg.shape, 1)
    lo = lane < HEAD_DIM
    zero = jnp.zeros_like(qg)
    return jnp.concatenate([jnp.where(lo, qg, zero), jnp.where(lo, zero, qg)], axis=0)


def _normalise_heads(acc, extra_den, tq):
    den = pltpu.roll(acc, HEAD_DIM, axis=1)
    if extra_den is not None:
        den = den + extra_den
    res = acc / den
    lane = lax.broadcasted_iota(jnp.int32, (tq, LANES), 1)
    return jnp.where(lane < HEAD_DIM, res[:tq], pltpu.roll(res[tq:], HEAD_DIM, axis=1))


def _gattn_kernel(q_ref, kl_ref, vl_ref, kc_ref, vc_ref, o_ref, *, tk):
    tq = q_ref.shape[1]
    n_lat = kl_ref.shape[1]
    n_ctx = kc_ref.shape[1]
    chunks = [(kl_ref, vl_ref, c * tk, tk) for c in range(n_lat // tk)] + [(kc_ref, vc_ref, 0, n_ctx)]
    for g in range(2):
        gs = slice(LANES * g, LANES * (g + 1))
        q2 = _stack_heads(q_ref[0, :, gs])
        m = jnp.full((2 * tq, 1), NEG_INF, F32)
        acc = jnp.zeros((2 * tq, LANES), F32)
        for k_ref, v_ref, off, size in chunks:
            s = _dot_nt(q2, k_ref[0, off:off + size, gs])
            m_new = jnp.maximum(m, jnp.max(s, axis=1, keepdims=True))
            p = jnp.exp2(s - m_new)
            acc = jnp.exp2(m - m_new) * acc + _dot(p.astype(BF16), v_ref[0, off:off + size, gs])
            m = m_new
        o_ref[0, :, gs] = _normalise_heads(acc, None, tq).astype(o_ref.dtype)


def _global_attention(q, k_lat, v_lat, k_ctx, v_ctx, tq, tk):
    bsz, n, _ = q.shape
    n_ctx = k_ctx.shape[1]
    full = lambda b, i: (b, 0, 0)
    return pl.pallas_call(
        functools.partial(_gattn_kernel, tk=tk),
        out_shape=jax.ShapeDtypeStruct((bsz, n, 256), BF16),
        grid=(bsz, n // tq),
        in_specs=[pl.BlockSpec((1, tq, 256), lambda b, i: (b, i, 0)),
                  pl.BlockSpec((1, n, 256), full), pl.BlockSpec((1, n, 256), full),
                  pl.BlockSpec((1, n_ctx, 256), full), pl.BlockSpec((1, n_ctx, 256), full)],
        out_specs=pl.BlockSpec((1, tq, 256), lambda b, i: (b, i, 0)),
        compiler_params=_cparams("parallel", "arbitrary"),
        name="global_attention",
    )(q, k_lat, v_lat, k_ctx, v_ctx)


def _sink_column(sink_ref, g, rows):
    row = lax.broadcasted_iota(jnp.int32, (2 * rows, 1), 0)
    return jnp.where(row < rows, sink_ref[2 * g], sink_ref[2 * g + 1]) * LOG2E


def _wattn_kernel(sink_ref, q_ref, kp_ref, kc_ref, kn_ref, vp_ref, vc_ref, vn_ref, kx_ref, vx_ref, o_ref,
                  kcat, vcat, *, nsub):
    i = pl.program_id(1)
    nb = pl.num_programs(1)
    w = WINDOW
    tq = nsub * w
    kcat[0:w] = kp_ref[0]
    kcat[w:w + tq] = kc_ref[0]
    kcat[w + tq:] = kn_ref[0]
    vcat[0:w] = vp_ref[0]
    vcat[w:w + tq] = vc_ref[0]
    vcat[w + tq:] = vn_ref[0]
    pw = 2 * w
    npair = nsub // 2
    rq = lax.broadcasted_iota(jnp.int32, (2 * pw, 2 * pw), 0) % pw
    col = lax.broadcasted_iota(jnp.int32, (2 * pw, 2 * pw), 1)
    band = (col >= rq) & (col <= rq + 2 * w)
    for g in range(2):
        gs = slice(LANES * g, LANES * (g + 1))
        q2_all = _stack_heads(q_ref[0, :, gs])
        s_ctx_all = _dot_nt(q2_all, kx_ref[0, :, gs])
        for p in range(npair):
            valid = band
            if p == 0:
                valid = valid & ((col >= w) | (i > 0))
            if p == npair - 1:
                valid = valid & ((col < 3 * w) | (i < nb - 1))
            r0 = slice(p * pw, (p + 1) * pw)
            r1 = slice(tq + p * pw, tq + (p + 1) * pw)
            keys = slice(p * pw, p * pw + 2 * pw)
            q2 = jnp.concatenate([q2_all[r0], q2_all[r1]], axis=0)
            s_loc = jnp.where(valid, _dot_nt(q2, kcat[keys, gs]), NEG_INF)
            s_ctx = jnp.concatenate([s_ctx_all[r0], s_ctx_all[r1]], axis=0)
            s_sink = _sink_column(sink_ref, g, pw)
            m = jnp.maximum(jnp.maximum(jnp.max(s_loc, axis=1, keepdims=True),
                                        jnp.max(s_ctx, axis=1, keepdims=True)), s_sink)
            p_loc = jnp.exp2(s_loc - m)
            p_ctx = jnp.exp2(s_ctx - m)
            acc = _dot(p_loc.astype(BF16), vcat[keys, gs]) + _dot(p_ctx.astype(BF16), vx_ref[0, :, gs])
            o_ref[0, r0, gs] = _normalise_heads(acc, jnp.exp2(s_sink - m), pw).astype(o_ref.dtype)


def _window_attention(sink, q, k, v, k_ctx, v_ctx, nsub):
    bsz, n, _ = q.shape
    n_ctx = k_ctx.shape[1]
    tq = nsub * Q_BLOCK
    nb128 = n // Q_BLOCK
    cur = lambda b, i: (b, i, 0)
    prev = lambda b, i: (b, jnp.maximum(i * nsub - 1, 0), 0)
    nxt = lambda b, i: (b, jnp.minimum((i + 1) * nsub, nb128 - 1), 0)
    full = lambda b, i: (b, 0, 0)
    halo = lambda im: pl.BlockSpec((1, Q_BLOCK, 256), im)
    big = pl.BlockSpec((1, tq, 256), cur)
    return pl.pallas_call(
        functools.partial(_wattn_kernel, nsub=nsub),
        out_shape=jax.ShapeDtypeStruct((bsz, n, 256), BF16),
        grid=(bsz, n // tq),
        in_specs=[pl.BlockSpec(memory_space=pltpu.SMEM),
                  big, halo(prev), big, halo(nxt), halo(prev), big, halo(nxt),
                  pl.BlockSpec((1, n_ctx, 256), full), pl.BlockSpec((1, n_ctx, 256), full)],
        out_specs=big,
        scratch_shapes=[pltpu.VMEM((tq + 2 * Q_BLOCK, 256), BF16), pltpu.VMEM((tq + 2 * Q_BLOCK, 256), BF16)],
        compiler_params=_cparams("parallel", "arbitrary"),
        name="window_attention",
    )(sink, q, k, k, k, v, v, v, k_ctx, v_ctx)


def _cattn_kernel(sink_ref, gq_ref, gk_ref, gv_ref, wq_ref, wk_ref, wv_ref, yb_ref, yc_ref):
    tq = gq_ref.shape[1]
    for g in range(2):
        gs = slice(LANES * g, LANES * (g + 1))
        s = _dot_nt(_stack_heads(gq_ref[0, :, gs]), gk_ref[0, :, gs])
        p = jnp.exp2(s - jnp.max(s, axis=1, keepdims=True))
        acc = _dot(p.astype(BF16), gv_ref[0, :, gs])
        yb_ref[0, :, gs] = _normalise_heads(acc, None, tq).astype(yb_ref.dtype)
        s = _dot_nt(_stack_heads(wq_ref[0, :, gs]), wk_ref[0, :, gs])
        s_sink = _sink_column(sink_ref, g, tq)
        m = jnp.maximum(jnp.max(s, axis=1, keepdims=True), s_sink)
        p = jnp.exp2(s - m)
        acc = _dot(p.astype(BF16), wv_ref[0, :, gs])
        yc_ref[0, :, gs] = _normalise_heads(acc, jnp.exp2(s_sink - m), tq).astype(yc_ref.dtype)


def _context_attention(sink, gq, gk, gv, wq, wk, wv):
    bsz, n_ctx, _ = gq.shape
    blk = pl.BlockSpec((1, n_ctx, 256), lambda b: (b, 0, 0))
    return pl.pallas_call(
        _cattn_kernel,
        out_shape=[jax.ShapeDtypeStruct((bsz, n_ctx, 256), BF16)] * 2,
        grid=(bsz,),
        in_specs=[pl.BlockSpec(memory_space=pltpu.SMEM)] + [blk] * 6,
        out_specs=[blk, blk],
        compiler_params=_cparams("parallel"),
        name="context_attention",
    )(sink, gq, gk, gv, wq, wk, wv)


def _ffn_kernel(*refs, tm, ck, final):
    if final:
        (ap_ref, ac_ref, an_ref, h1_ref, wu_ref, cw_ref, cb_ref, wd_ref, mod_ref, gf_ref,
         o_ref, ext_scr, up_scr, act_scr) = refs
    else:
        (ap_ref, ac_ref, an_ref, h1_ref, wu_ref, cw_ref, cb_ref, wd_ref, mod_ref,
         o_ref, ext_scr, up_scr, act_scr) = refs
    i = pl.program_id(1)
    nt = pl.num_programs(1)
    h0 = BF16_ROWS
    ext_scr[0:h0] = jnp.where(i > 0, ap_ref[0], jnp.zeros_like(ap_ref[0]))
    ext_scr[h0:h0 + tm] = ac_ref[0]
    ext_scr[h0 + tm:] = jnp.where(i < nt - 1, an_ref[0], jnp.zeros_like(an_ref[0]))
    n_chunks = wu_ref.shape[0]

    def up_proj(c, slot):
        up_scr[slot] = _dot(ext_scr[...], wu_ref[c])

    def conv_act(c, slot):
        up = up_scr[slot]
        cw = cw_ref[c]
        u = cb_ref[c] + cw[0:1] * up[h0 - 1:h0 - 1 + tm] + cw[1:2] * up[h0:h0 + tm] + cw[2:3] * up[h0 + 1:h0 + 1 + tm]
        act_scr[c] = (_silu(u[:, :ck]) * u[:, ck:]).astype(BF16)

    up_proj(0, 0)
    n_pairs = (n_chunks - 1) // 2

    def pair(j, _):
        c = 2 * j
        up_proj(c + 1, 1)
        conv_act(c, 0)
        up_proj(c + 2, 0)
        conv_act(c + 1, 1)
        return 0

    for j in range(n_pairs):
        pair(j, 0)
    c = 2 * n_pairs
    if c + 1 < n_chunks:
        up_proj(c + 1, 1)
    conv_act(c, 0)
    if c + 1 < n_chunks:
        conv_act(c + 1, 1)

    acc = _dot(act_scr[0], wd_ref[0])
    for c in range(1, n_chunks):
        acc = acc + _dot(act_scr[c], wd_ref[c])
    h2 = h1_ref[0] + mod_ref[0, 5:6, :] * acc
    if final:
        h2 = _rms_rows(h2, gf_ref[...])
    o_ref[0] = h2


def _conv_ffn(a2, h1, wu, cw, cb, wd, mod, mod_row, final_g, tm):
    bsz, n, d = h1.shape
    n_chunks, _, ck2 = wu.shape
    ck = ck2 // 2
    hb = tm // BF16_ROWS
    nhb = n // BF16_ROWS
    final = final_g is not None
    tok = lambda b, i: (b, i, 0)
    prev = lambda b, i: (b, jnp.maximum(i * hb - 1, 0), 0)
    nxt = lambda b, i: (b, jnp.minimum((i + 1) * hb, nhb - 1), 0)
    const3 = lambda b, i: (0, 0, 0)
    single = pl.Buffered(1)
    in_specs = [pl.BlockSpec((1, BF16_ROWS, d), prev),
                pl.BlockSpec((1, tm, d), tok),
                pl.BlockSpec((1, BF16_ROWS, d), nxt),
                pl.BlockSpec((1, tm, d), tok),
                pl.BlockSpec(wu.shape, const3, pipeline_mode=single),
                pl.BlockSpec(cw.shape, const3),
                pl.BlockSpec(cb.shape, const3),
                pl.BlockSpec(wd.shape, const3, pipeline_mode=single),
                pl.BlockSpec((1, 6, d), lambda b, i: (mod_row(b), 0, 0))]
    args = [a2, a2, a2, h1, wu, cw, cb, wd, mod]
    if final:
        in_specs.append(pl.BlockSpec((1, d), lambda b, i: (0, 0)))
        args.append(final_g)
    return pl.pallas_call(
        functools.partial(_ffn_kernel, tm=tm, ck=ck, final=final),
        out_shape=jax.ShapeDtypeStruct((bsz, n, d), F32),
        grid=(bsz, n // tm),
        in_specs=in_specs,
        out_specs=pl.BlockSpec((1, tm, d), tok),
        scratch_shapes=[pltpu.VMEM((tm + 2 * BF16_ROWS, d), BF16),
                        pltpu.VMEM((2, tm + 2 * BF16_ROWS, ck2), F32),
                        pltpu.VMEM((n_chunks, tm, ck), BF16)],
        compiler_params=_cparams("parallel", "parallel"),
        name="conv_ffn_final" if final else "conv_ffn",
    )(*args)


def _rope_tables(n):
    pos = jnp.arange(n)
    row = (pos // GRID_W).astype(F32)
    col = (pos % GRID_W).astype(F32)
    n_freq = HEAD_DIM // 4
    inv = ROPE_BASE ** (-jnp.arange(n_freq, dtype=F32) / n_freq)
    ang = jnp.concatenate([row[:, None] * inv, col[:, None] * inv], axis=-1)
    cos, sin = jnp.cos(ang), jnp.sin(ang)
    cos_h = jnp.concatenate([cos, cos], axis=-1)
    sin_h = jnp.concatenate([-sin, sin], axis=-1)
    return jnp.tile(cos_h, (1, 4)), jnp.tile(sin_h, (1, 4))


def _block_diag(w):
    nb, c, d = w.shape
    eye = jnp.eye(nb, dtype=w.dtype)
    return jnp.einsum("ncd,nm->ncmd", w, eye).reshape(nb * c, nb * d)


def _ffn_chunks(d_ff):
    for ck in (256, 128):
        if d_ff % ck == 0:
            return ck
    raise ValueError("d_ff must be a multiple of 128")


def kernel(x, c, ctx, c_ctx, w_mod, b_mod, norm1_g, w_in, lru_conv_w, lru_conv_b, lru_w_a, lru_b_a, lru_w_x, lru_b_x, lru_lam, ga_q_norm_g, ga_k_norm_g, wa_sink, w_out, norm2_g, w_up, ffn_conv_w, ffn_conv_b, w_down, final_norm_g):
    bsz, n, d = x.shape
    n_ctx = ctx.shape[1]
    depth = w_in.shape[0]
    d_lru = lru_conv_w.shape[-1]
    d_ff = w_down.shape[1]
    assert w_in.shape[2] == 2 * d_lru + 1024 and n % 512 == 0 and n_ctx % 128 == 0 and bsz <= 16

    rows = 24
    cvec = jnp.zeros((rows, d), F32).at[:bsz].set(c).at[bsz].set(c_ctx)
    mod_all = _modulation(cvec, w_mod, b_mod).reshape(depth, rows, 6, d)
    lat_row = lambda b: b
    ctx_row = lambda b: bsz

    cos, sin = _rope_tables(n)
    seg = _block_diag(jnp.ones((4, HEAD_DIM, HEAD_DIM), BF16))
    ck = _ffn_chunks(d_ff)
    n_ck = d_ff // ck

    tm_lat, tm_ctx = 512, n_ctx
    h, hc = x, ctx
    for l in range(depth):
        need_ctx = l < depth - 1
        mod = mod_all[l]
        g1 = norm1_g[l].reshape(1, d)
        g2 = norm2_g[l].reshape(1, d)
        w_in_l = w_in[l].astype(BF16)
        qg = jnp.tile(ga_q_norm_g[l], 4).reshape(1, 256)
        kg = jnp.tile(ga_k_norm_g[l], 2).reshape(1, 128)
        sink = wa_sink[l]
        conv_w = lru_conv_w[l]
        conv_b = lru_conv_b[l].reshape(1, d_lru)
        w_out_l = w_out[l].astype(BF16)
        wu = w_up[l].astype(BF16)
        wu = jnp.concatenate([wu[:, :d_ff].reshape(d, n_ck, ck), wu[:, d_ff:].reshape(d, n_ck, ck)], axis=-1)
        wu = wu.transpose(1, 0, 2)
        fcw = ffn_conv_w[l]
        fcw = jnp.concatenate([fcw[:, :d_ff].reshape(-1, n_ck, ck), fcw[:, d_ff:].reshape(-1, n_ck, ck)], axis=-1)
        fcw = fcw.transpose(1, 0, 2)
        fcb = ffn_conv_b[l]
        fcb = jnp.concatenate([fcb[:d_ff].reshape(n_ck, 1, ck), fcb[d_ff:].reshape(n_ck, 1, ck)], axis=-1)
        wd = w_down[l].astype(BF16).reshape(n_ck, ck, d)

        wg = [(0.5 * jnp.concatenate([_block_diag(lru_w_a[l, dr]), _block_diag(lru_w_x[l, dr])], axis=1)).astype(BF16)
              for dr in range(2)]
        bg = [0.5 * jnp.concatenate([lru_b_a[l, dr], lru_b_x[l, dr]]).reshape(1, 2 * d_lru) for dr in range(2)]
        lam = [lru_lam[l, dr].reshape(1, d_lru) for dr in range(2)]
        zero_state = jnp.zeros((bsz, 1, d_lru), F32)

        u_c, gate_c, hf_c, last_f, gq_c, gk_c, gv_c, wq_c, wk_c, wv_c = _in_projection(
            hc, mod, ctx_row, g1, w_in_l, seg, qg, kg, None, None,
            conv_w, conv_b, wg[0], bg[0], lam[0], zero_state, tm_ctx)
        u_l, gate_l, hf_l, _, gq_l, gk_l, gv_l, wq_l, wk_l, wv_l = _in_projection(
            h, mod, lat_row, g1, w_in_l, seg, qg, kg, cos, sin,
            conv_w, conv_b, wg[0], bg[0], lam[0], last_f, tm_lat)

        yb_l = _global_attention(gq_l, gk_l, gv_l, gk_c, gv_c, 512, 512)
        yc_l = _window_attention(sink, wq_l, wk_l, wv_l, wk_c, wv_c, 4)

        if need_ctx:
            yb_c, yc_c = _context_attention(sink, gq_c, gk_c, gv_c, wq_c, wk_c, wv_c)
            hc1, ac2, last_b = _lru_backward(u_c, wg[1], bg[1], lam[1], zero_state, n_ctx,
                                             proj=(hf_c, gate_c, hc, yb_c, yc_c, w_out_l, mod, ctx_row, g2))
        else:
            (last_b,) = _lru_backward(u_c, wg[1], bg[1], lam[1], zero_state, n_ctx)
        h1, a2, _ = _lru_backward(u_l, wg[1], bg[1], lam[1], last_b, 512,
                                  proj=(hf_l, gate_l, h, yb_l, yc_l, w_out_l, mod, lat_row, g2))

        final_g = None if need_ctx else final_norm_g.reshape(1, d)
        h = _conv_ffn(a2, h1, wu, fcw, fcb, wd, mod, lat_row, final_g, tm_lat)
        if need_ctx:
            hc = _conv_ffn(ac2, hc1, wu, fcw, fcb, wd, mod, ctx_row, None, tm_ctx)
    return h
```

```python
import functools

import jax
import jax.numpy as jnp
from jax import lax
from jax.experimental import pallas as pl
from jax.experimental.pallas import tpu as pltpu

F32 = jnp.float32
BF16 = jnp.bfloat16

HEAD_DIM = 64
GRID_W = 64
WINDOW = 128
Q_BLOCK = 128
LRU_C = 8.0
ROPE_BASE = 10000.0
EPS = 1e-6
NEG_INF = -1e30
ATTN_SCALE = HEAD_DIM ** -0.5
LOG2E = 1.4426950408889634
Q_SCALE = ATTN_SCALE * LOG2E
LANES = 128
SUBLANES = 8
BF16_ROWS = 16
VMEM_LIMIT = 56 * 1024 * 1024


def _cparams(*sem):
    return pltpu.CompilerParams(dimension_semantics=sem, vmem_limit_bytes=VMEM_LIMIT)


def _rms_rows(x, g):
    ms = jnp.mean(x * x, axis=-1, keepdims=True)
    return x * lax.rsqrt(ms + EPS) * g


def _ada_norm(x, g, shift, scale):
    ms = jnp.mean(x * x, axis=-1, keepdims=True)
    return (x * lax.rsqrt(ms + EPS)) * (g * (1.0 + scale)) + shift


def _silu(x):
    half = 0.5 * x
    return half * (1.0 + jnp.tanh(half))


def _gelu_tanh(x):
    c = 0.7978845608028654
    return 0.5 * x * (1.0 + jnp.tanh(c * (x + 0.044715 * (x * x * x))))


def _dot(a, b):
    return jnp.dot(a, b, preferred_element_type=F32)


def _dot_nt(a, b):
    return lax.dot_general(a, b, (((1,), (1,)), ((), ())), preferred_element_type=F32)


def _mod_kernel(c_ref, w_ref, b_ref, o_ref):
    s = _silu(c_ref[...])
    o_ref[0] = jnp.dot(s, w_ref[0], preferred_element_type=F32,
                       precision=lax.Precision.HIGHEST) + b_ref[0]


def _modulation(cvec, w_mod, b_mod):
    n_layers, d, d6 = w_mod.shape
    rows = cvec.shape[0]
    tn = 1536
    return pl.pallas_call(
        _mod_kernel,
        out_shape=jax.ShapeDtypeStruct((n_layers, rows, d6), F32),
        grid=(n_layers, d6 // tn),
        in_specs=[pl.BlockSpec((rows, d), lambda l, j: (0, 0)),
                  pl.BlockSpec((1, d, tn), lambda l, j: (l, 0, j)),
                  pl.BlockSpec((1, 1, tn), lambda l, j: (l, 0, j))],
        out_specs=pl.BlockSpec((1, rows, tn), lambda l, j: (l, 0, j)),
        compiler_params=_cparams("parallel", "parallel"),
        name="modulation",
    )(cvec, w_mod, b_mod.reshape(n_layers, 1, d6))


def _seg_mean_sq(x, seg):
    sq = x * x
    hi = sq.astype(BF16)
    lo = (sq - hi.astype(F32)).astype(BF16)
    return (_dot(hi, seg) + _dot(lo, seg)) * (1.0 / HEAD_DIM)


def _swap_halves(x):
    w = x.shape[-1]
    lane = lax.broadcasted_iota(jnp.int32, x.shape, 1)
    first = (lane % HEAD_DIM) < (HEAD_DIM // 2)
    up = pltpu.roll(x, w - HEAD_DIM // 2, axis=1)
    down = pltpu.roll(x, HEAD_DIM // 2, axis=1)
    return jnp.where(first, up, down)


def _dup_kv_heads(k):
    lane = lax.broadcasted_iota(jnp.int32, k.shape, 1)
    lo = lane < HEAD_DIM
    sw = pltpu.roll(k, HEAD_DIM, axis=1)
    return jnp.concatenate([jnp.where(lo, k, sw), jnp.where(lo, sw, k)], axis=1)


def _aug_kv_heads(v):
    lane = lax.broadcasted_iota(jnp.int32, v.shape, 1)
    lo = lane < HEAD_DIM
    sw = pltpu.roll(v, HEAD_DIM, axis=1)
    return jnp.concatenate([jnp.where(lo, v, 1.0), jnp.where(lo, sw, 1.0)], axis=1)


def _sqrt_nonneg(x):
    return x * lax.rsqrt(jnp.maximum(x, 1e-30))


def _lru_gates(u, wg_ref, bg_ref, lam_ref, a_scr, d_scr):
    d_lru = u.shape[-1]
    th = jnp.tanh(_dot(u.astype(BF16), wg_ref[...]) + bg_ref[...])
    nlam = -lam_ref[...]
    sp = jnp.maximum(nlam, 0.0) + jnp.log1p(jnp.exp(-jnp.abs(nlam)))
    half_c = (-0.5 * LRU_C * LOG2E) * sp
    a = jnp.exp2(half_c * th[:, :d_lru] + half_c)
    i_g = 0.5 * th[:, d_lru:] + 0.5
    a_scr[...] = a
    d_scr[...] = _sqrt_nonneg(1.0 - a * a) * i_g * u


def _lru_scan(a_scr, d_scr, carry_scr, reverse, unroll=2):
    tl, d_lru = a_scr.shape
    row = lax.broadcasted_iota(jnp.int32, (SUBLANES, d_lru), 0)
    n_chunks = tl // SUBLANES

    def chunk(c, carry):
        cc = (n_chunks - 1 - c) if reverse else c
        off = pl.multiple_of(cc * SUBLANES, SUBLANES)
        av = a_scr[pl.ds(off, SUBLANES), :]
        dv = d_scr[pl.ds(off, SUBLANES), :]
        for s in (1, 2, 4):
            if reverse:
                a_sh = pltpu.roll(av, SUBLANES - s, axis=0)
                d_sh = pltpu.roll(dv, SUBLANES - s, axis=0)
                ok = row < SUBLANES - s
            else:
                a_sh = pltpu.roll(av, s, axis=0)
                d_sh = pltpu.roll(dv, s, axis=0)
                ok = row >= s
            dv = jnp.where(ok, av * d_sh + dv, dv)
            av = jnp.where(ok, av * a_sh, av)
        hv = av * carry + dv
        d_scr[pl.ds(off, SUBLANES), :] = hv
        edge = hv[0:1, :] if reverse else hv[SUBLANES - 1:SUBLANES, :]
        return jnp.broadcast_to(edge, (SUBLANES, d_lru))

    carry = lax.fori_loop(0, n_chunks, chunk, carry_scr[...], unroll=unroll)
    carry_scr[...] = carry
    return carry


def _inproj_kernel(*refs, rope, d_lru, tm):
    if rope:
        (h_ref, hn_ref, mod_ref, g_ref, w_ref, seg_ref, qg_ref, kg_ref, cos_ref, sin_ref,
         cw_ref, cb_ref, wg_ref, bg_ref, lam_ref, h0_ref,
         u_ref, gate_ref, hf_ref, last_ref, gq_ref, gk_ref, gv_ref, wq_ref, wk_ref, wv_ref,
         tail_scr, a_scr, d_scr, carry_scr) = refs
    else:
        (h_ref, hn_ref, mod_ref, g_ref, w_ref, seg_ref, qg_ref, kg_ref,
         cw_ref, cb_ref, wg_ref, bg_ref, lam_ref, h0_ref,
         u_ref, gate_ref, hf_ref, last_ref, gq_ref, gk_ref, gv_ref, wq_ref, wk_ref, wv_ref,
         tail_scr, a_scr, d_scr, carry_scr) = refs
    i = pl.program_id(1)
    nt = pl.num_programs(1)

    @pl.when(i == 0)
    def _():
        carry_scr[...] = jnp.broadcast_to(h0_ref[0], carry_scr.shape)
        tail_scr[...] = jnp.zeros_like(tail_scr)

    x = jnp.concatenate([h_ref[0], hn_ref[0]], axis=0)
    a = _ada_norm(x, g_ref[...], mod_ref[0, 0:1, :], mod_ref[0, 1:2, :])
    y_ext = _dot(a.astype(BF16), w_ref[...])
    y = y_ext[:tm]
    o = 0
    r = y[:, o:o + d_lru]; o += d_lru
    gate_ref[0] = y[:, o:o + d_lru]; o += d_lru
    gq = y[:, o:o + 256]; o += 256
    gk = y[:, o:o + 128]; o += 128
    gv = y[:, o:o + 128]; o += 128
    wq = y[:, o:o + 256]; o += 256
    wk = y[:, o:o + 128]; o += 128
    wv = y[:, o:o + 128]

    seg = seg_ref[...]
    gq = gq * lax.rsqrt(_seg_mean_sq(gq, seg) + EPS) * qg_ref[...]
    gk = gk * lax.rsqrt(_seg_mean_sq(gk, seg[:128, :128]) + EPS) * kg_ref[...]
    if rope:
        cos = cos_ref[...]
        sin = sin_ref[...]
        gq = gq * cos + _swap_halves(gq) * sin
        wq = wq * cos + _swap_halves(wq) * sin
        gk = gk * cos[:, :128] + _swap_halves(gk) * sin[:, :128]
        wk = wk * cos[:, :128] + _swap_halves(wk) * sin[:, :128]
    gq_ref[0] = (gq * Q_SCALE).astype(BF16)
    wq_ref[0] = (wq * Q_SCALE).astype(BF16)
    gk_ref[0] = _dup_kv_heads(gk).astype(BF16)
    gv_ref[0] = _aug_kv_heads(gv).astype(BF16)
    wk_ref[0] = _dup_kv_heads(wk).astype(BF16)
    wv_ref[0] = _aug_kv_heads(wv).astype(BF16)

    has_next = (i < nt - 1).astype(F32)
    ext = jnp.concatenate([tail_scr[...], r, y_ext[tm:, :d_lru] * has_next], axis=0)
    tail_scr[...] = r[tm - SUBLANES:]
    rows = tm + 2 * SUBLANES
    lo, hi = SUBLANES, SUBLANES + tm
    cw = cw_ref[...]
    u = cb_ref[...] + cw[2:3] * ext[lo:hi]
    u = u + cw[0:1] * pltpu.roll(ext, 2, axis=0)[lo:hi]
    u = u + cw[1:2] * pltpu.roll(ext, 1, axis=0)[lo:hi]
    u = u + cw[3:4] * pltpu.roll(ext, rows - 1, axis=0)[lo:hi]
    u_ref[0] = u

    _lru_gates(u, wg_ref, bg_ref, lam_ref, a_scr, d_scr)
    carry = _lru_scan(a_scr, d_scr, carry_scr, False, unroll=True)
    hf_ref[0] = d_scr[...]
    last_ref[0] = carry[0:1, :]


def _in_projection(h, mod, mod_row, g1, w_in, seg, qg, kg, cos, sin, conv_w, conv_b, wg, bg, lam, h0, tm):
    bsz, n, d = h.shape
    d_in = w_in.shape[1]
    d_lru = conv_w.shape[-1]
    rope = cos is not None
    nhb = n // SUBLANES
    tok = lambda b, i: (b, i, 0)
    nxt = lambda b, i: (b, jnp.minimum((i + 1) * (tm // SUBLANES), nhb - 1), 0)
    const2 = lambda b, i: (0, 0)
    state_spec = pl.BlockSpec((1, 1, d_lru), lambda b, i: (b, 0, 0))
    in_specs = [pl.BlockSpec((1, tm, d), tok),
                pl.BlockSpec((1, SUBLANES, d), nxt),
                pl.BlockSpec((1, 6, d), lambda b, i: (mod_row(b), 0, 0)),
                pl.BlockSpec((1, d), const2),
                pl.BlockSpec((d, d_in), const2),
                pl.BlockSpec((256, 256), const2),
                pl.BlockSpec((1, 256), const2),
                pl.BlockSpec((1, 128), const2)]
    args = [h, h, mod, g1, w_in, seg, qg, kg]
    if rope:
        in_specs += [pl.BlockSpec((tm, 256), lambda b, i: (i, 0)),
                     pl.BlockSpec((tm, 256), lambda b, i: (i, 0))]
        args += [cos, sin]
    in_specs += [pl.BlockSpec(conv_w.shape, const2),
                 pl.BlockSpec((1, d_lru), const2),
                 pl.BlockSpec(wg.shape, const2),
                 pl.BlockSpec((1, 2 * d_lru), const2),
                 pl.BlockSpec((1, d_lru), const2),
                 state_spec]
    args += [conv_w, conv_b, wg, bg, lam, h0]
    out_shape = ([jax.ShapeDtypeStruct((bsz, n, d_lru), F32)] * 3
                 + [jax.ShapeDtypeStruct((bsz, 1, d_lru), F32)]
                 + [jax.ShapeDtypeStruct((bsz, n, 256), BF16)] * 6)
    out_specs = ([pl.BlockSpec((1, tm, d_lru), tok)] * 3 + [state_spec]
                 + [pl.BlockSpec((1, tm, 256), tok)] * 6)
    return pl.pallas_call(
        functools.partial(_inproj_kernel, rope=rope, d_lru=d_lru, tm=tm),
        out_shape=out_shape,
        grid=(bsz, n // tm),
        in_specs=in_specs,
        out_specs=out_specs,
        scratch_shapes=[pltpu.VMEM((SUBLANES, d_lru), F32), pltpu.VMEM((tm, d_lru), F32),
                        pltpu.VMEM((tm, d_lru), F32), pltpu.VMEM((SUBLANES, d_lru), F32)],
        compiler_params=_cparams("parallel", "arbitrary"),
        name="in_projection_lat" if rope else "in_projection_ctx",
    )(*args)


def _lru_bwd_kernel(*refs, project):
    if project:
        (u_ref, wg_ref, bg_ref, lam_ref, h0_ref, hf_ref, gate_ref,
         h_ref, yb_ref, yc_ref, wa_ref, wb_ref, wc_ref, mod_ref, g2_ref,
         h1_ref, a2_ref, last_ref, a_scr, d_scr, carry_scr) = refs
    else:
        (u_ref, wg_ref, bg_ref, lam_ref, h0_ref, last_ref, a_scr, d_scr, carry_scr) = refs
    j = pl.program_id(1)

    @pl.when(j == 0)
    def _():
        carry_scr[...] = jnp.broadcast_to(h0_ref[0], carry_scr.shape)

    _lru_gates(u_ref[0], wg_ref, bg_ref, lam_ref, a_scr, d_scr)
    carry = _lru_scan(a_scr, d_scr, carry_scr, True, unroll=True)
    last_ref[0] = carry[0:1, :]
    if not project:
        return
    ya = ((d_scr[...] + hf_ref[0]) * _gelu_tanh(gate_ref[0])).astype(BF16)
    proj = _dot(ya, wa_ref[...]) + _dot(yb_ref[0], wb_ref[...]) + _dot(yc_ref[0], wc_ref[...])
    h1 = h_ref[0] + mod_ref[0, 2:3, :] * proj
    h1_ref[0] = h1
    a2 = _ada_norm(h1, g2_ref[...], mod_ref[0, 3:4, :], mod_ref[0, 4:5, :])
    a2_ref[0] = a2.astype(a2_ref.dtype)


def _lru_backward(u, wg, bg, lam, h0, tl, proj=None):
    bsz, n, c = u.shape
    nt = n // tl
    cur = lambda b, j: (b, nt - 1 - j, 0)
    const2 = lambda b, j: (0, 0)
    state_spec = pl.BlockSpec((1, 1, c), lambda b, j: (b, 0, 0))
    in_specs = [pl.BlockSpec((1, tl, c), cur),
                pl.BlockSpec(wg.shape, const2),
                pl.BlockSpec((1, 2 * c), const2),
                pl.BlockSpec((1, c), const2),
                state_spec]
    args = [u, wg, bg, lam, h0]
    out_shape, out_specs = [], []
    if proj is not None:
        hf, gate, h, yb, yc, w_out, mod, mod_row, g2 = proj
        d = h.shape[-1]
        cb_, cc_ = yb.shape[-1], yc.shape[-1]
        in_specs += [pl.BlockSpec((1, tl, c), cur), pl.BlockSpec((1, tl, c), cur),
                     pl.BlockSpec((1, tl, d), cur),
                     pl.BlockSpec((1, tl, cb_), cur), pl.BlockSpec((1, tl, cc_), cur),
                     pl.BlockSpec((c, d), const2), pl.BlockSpec((cb_, d), const2), pl.BlockSpec((cc_, d), const2),
                     pl.BlockSpec((1, 6, d), lambda b, j: (mod_row(b), 0, 0)),
                     pl.BlockSpec((1, d), const2)]
        args += [hf, gate, h, yb, yc, w_out[:c], w_out[c:c + cb_], w_out[c + cb_:], mod, g2]
        out_shape = [jax.ShapeDtypeStruct((bsz, n, d), F32), jax.ShapeDtypeStruct((bsz, n, d), BF16)]
        out_specs = [pl.BlockSpec((1, tl, d), cur), pl.BlockSpec((1, tl, d), cur)]
    return pl.pallas_call(
        functools.partial(_lru_bwd_kernel, project=proj is not None),
        out_shape=out_shape + [jax.ShapeDtypeStruct((bsz, 1, c), F32)],
        grid=(bsz, nt),
        in_specs=in_specs,
        out_specs=out_specs + [state_spec],
        scratch_shapes=[pltpu.VMEM((tl, c), F32), pltpu.VMEM((tl, c), F32),
                        pltpu.VMEM((SUBLANES, c), F32)],
        compiler_params=_cparams("parallel", "arbitrary"),
        name="lru_bwd_project" if proj is not None else "lru_bwd_state",
    )(*args)


def _stack_heads(qg):
    lane = lax.broadcasted_iota(jnp.int32, qg.shape, 1)
    lo = lane < HEAD_DIM
    zero = jnp.zeros_like(qg)
    return jnp.concatenate([jnp.where(lo, qg, zero), jnp.where(lo, zero, qg)], axis=0)


def _normalise_heads(acc, extra_den, tq):
    den = pltpu.roll(acc, HEAD_DIM, axis=1)
    if extra_den is not None:
        den = den + extra_den
    res = acc / den
    lane = lax.broadcasted_iota(jnp.int32, (tq, LANES), 1)
    return jnp.where(lane < HEAD_DIM, res[:tq], pltpu.roll(res[tq:], HEAD_DIM, axis=1))


def _gattn_kernel(q_ref, kl_ref, vl_ref, kc_ref, vc_ref, o_ref, *, tk):
    tq = q_ref.shape[1]
    n_lat = kl_ref.shape[1]
    n_ctx = kc_ref.shape[1]
    chunks = [(kl_ref, vl_ref, c * tk, tk) for c in range(n_lat // tk)] + [(kc_ref, vc_ref, 0, n_ctx)]
    for g in range(2):
        gs = slice(LANES * g, LANES * (g + 1))
        q2 = _stack_heads(q_ref[0, :, gs])
        m = jnp.full((2 * tq, 1), NEG_INF, F32)
        acc = jnp.zeros((2 * tq, LANES), F32)
        for k_ref, v_ref, off, size in chunks:
            s = _dot_nt(q2, k_ref[0, off:off + size, gs])
            m_new = jnp.maximum(m, jnp.max(s, axis=1, keepdims=True))
            p = jnp.exp2(s - m_new)
            acc = jnp.exp2(m - m_new) * acc + _dot(p.astype(BF16), v_ref[0, off:off + size, gs])
            m = m_new
        o_ref[0, :, gs] = _normalise_heads(acc, None, tq).astype(o_ref.dtype)


def _global_attention(q, k_lat, v_lat, k_ctx, v_ctx, tq, tk):
    bsz, n, _ = q.shape
    n_ctx = k_ctx.shape[1]
    full = lambda b, i: (b, 0, 0)
    return pl.pallas_call(
        functools.partial(_gattn_kernel, tk=tk),
        out_shape=jax.ShapeDtypeStruct((bsz, n, 256), BF16),
        grid=(bsz, n // tq),
        in_specs=[pl.BlockSpec((1, tq, 256), lambda b, i: (b, i, 0)),
                  pl.BlockSpec((1, n, 256), full), pl.BlockSpec((1, n, 256), full),
                  pl.BlockSpec((1, n_ctx, 256), full), pl.BlockSpec((1, n_ctx, 256), full)],
        out_specs=pl.BlockSpec((1, tq, 256), lambda b, i: (b, i, 0)),
        compiler_params=_cparams("parallel", "arbitrary"),
        name="global_attention",
    )(q, k_lat, v_lat, k_ctx, v_ctx)


def _sink_column(sink_ref, g, rows):
    row = lax.broadcasted_iota(jnp.int32, (2 * rows, 1), 0)
    return jnp.where(row < rows, sink_ref[2 * g], sink_ref[2 * g + 1]) * LOG2E


def _wattn_kernel(sink_ref, q_ref, kp_ref, kc_ref, kn_ref, vp_ref, vc_ref, vn_ref, kx_ref, vx_ref, o_ref,
                  kcat, vcat, *, nsub):
    i = pl.program_id(1)
    nb = pl.num_programs(1)
    w = WINDOW
    tq = nsub * w
    kcat[0:w] = kp_ref[0]
    kcat[w:w + tq] = kc_ref[0]
    kcat[w + tq:] = kn_ref[0]
    vcat[0:w] = vp_ref[0]
    vcat[w:w + tq] = vc_ref[0]
    vcat[w + tq:] = vn_ref[0]
    pw = 2 * w
    npair = nsub // 2
    rq = lax.broadcasted_iota(jnp.int32, (2 * pw, 2 * pw), 0) % pw
    col = lax.broadcasted_iota(jnp.int32, (2 * pw, 2 * pw), 1)
    band = (col >= rq) & (col <= rq + 2 * w)
    for g in range(2):
        gs = slice(LANES * g, LANES * (g + 1))
        q2_all = _stack_heads(q_ref[0, :, gs])
        s_ctx_all = _dot_nt(q2_all, kx_ref[0, :, gs])
        for p in range(npair):
            valid = band
            if p == 0:
                valid = valid & ((col >= w) | (i > 0))
            if p == npair - 1:
                valid = valid & ((col < 3 * w) | (i < nb - 1))
            r0 = slice(p * pw, (p + 1) * pw)
            r1 = slice(tq + p * pw, tq + (p + 1) * pw)
            keys = slice(p * pw, p * pw + 2 * pw)
            q2 = jnp.concatenate([q2_all[r0], q2_all[r1]], axis=0)
            s_loc = jnp.where(valid, _dot_nt(q2, kcat[keys, gs]), NEG_INF)
            s_ctx = jnp.concatenate([s_ctx_all[r0], s_ctx_all[r1]], axis=0)
            s_sink = _sink_column(sink_ref, g, pw)
            m = jnp.maximum(jnp.maximum(jnp.max(s_loc, axis=1, keepdims=True),
                                        jnp.max(s_ctx, axis=1, keepdims=True)), s_sink)
            p_loc = jnp.exp2(s_loc - m)
            p_ctx = jnp.exp2(s_ctx - m)
            acc = _dot(p_loc.astype(BF16), vcat[keys, gs]) + _dot(p_ctx.astype(BF16), vx_ref[0, :, gs])
            o_ref[0, r0, gs] = _normalise_heads(acc, jnp.exp2(s_sink - m), pw).astype(o_ref.dtype)


def _window_attention(sink, q, k, v, k_ctx, v_ctx, nsub):
    bsz, n, _ = q.shape
    n_ctx = k_ctx.shape[1]
    tq = nsub * Q_BLOCK
    nb128 = n // Q_BLOCK
    cur = lambda b, i: (b, i, 0)
    prev = lambda b, i: (b, jnp.maximum(i * nsub - 1, 0), 0)
    nxt = lambda b, i: (b, jnp.minimum((i + 1) * nsub, nb128 - 1), 0)
    full = lambda b, i: (b, 0, 0)
    halo = lambda im: pl.BlockSpec((1, Q_BLOCK, 256), im)
    big = pl.BlockSpec((1, tq, 256), cur)
    return pl.pallas_call(
        functools.partial(_wattn_kernel, nsub=nsub),
        out_shape=jax.ShapeDtypeStruct((bsz, n, 256), BF16),
        grid=(bsz, n // tq),
        in_specs=[pl.BlockSpec(memory_space=pltpu.SMEM),
                  big, halo(prev), big, halo(nxt), halo(prev), big, halo(nxt),
                  pl.BlockSpec((1, n_ctx, 256), full), pl.BlockSpec((1, n_ctx, 256), full)],
        out_specs=big,
        scratch_shapes=[pltpu.VMEM((tq + 2 * Q_BLOCK, 256), BF16), pltpu.VMEM((tq + 2 * Q_BLOCK, 256), BF16)],
        compiler_params=_cparams("parallel", "arbitrary"),
        name="window_attention",
    )(sink, q, k, k, k, v, v, v, k_ctx, v_ctx)


def _cattn_kernel(sink_ref, gq_ref, gk_ref, gv_ref, wq_ref, wk_ref, wv_ref, yb_ref, yc_ref):
    tq = gq_ref.shape[1]
    for g in range(2):
        gs = slice(LANES * g, LANES * (g + 1))
        s = _dot_nt(_stack_heads(gq_ref[0, :, gs]), gk_ref[0, :, gs])
        p = jnp.exp2(s - jnp.max(s, axis=1, keepdims=True))
        acc = _dot(p.astype(BF16), gv_ref[0, :, gs])
        yb_ref[0, :, gs] = _normalise_heads(acc, None, tq).astype(yb_ref.dtype)
        s = _dot_nt(_stack_heads(wq_ref[0, :, gs]), wk_ref[0, :, gs])
        s_sink = _sink_column(sink_ref, g, tq)
        m = jnp.maximum(jnp.max(s, axis=1, keepdims=True), s_sink)
        p = jnp.exp2(s - m)
        acc = _dot(p.astype(BF16), wv_ref[0, :, gs])
        yc_ref[0, :, gs] = _normalise_heads(acc, jnp.exp2(s_sink - m), tq).astype(yc_ref.dtype)


def _context_attention(sink, gq, gk, gv, wq, wk, wv):
    bsz, n_ctx, _ = gq.shape
    blk = pl.BlockSpec((1, n_ctx, 256), lambda b: (b, 0, 0))
    return pl.pallas_call(
        _cattn_kernel,
        out_shape=[jax.ShapeDtypeStruct((bsz, n_ctx, 256), BF16)] * 2,
        grid=(bsz,),
        in_specs=[pl.BlockSpec(memory_space=pltpu.SMEM)] + [blk] * 6,
        out_specs=[blk, blk],
        compiler_params=_cparams("parallel"),
        name="context_attention",
    )(sink, gq, gk, gv, wq, wk, wv)


def _ffn_kernel(*refs, tm, ck, final):
    if final:
        (ap_ref, ac_ref, an_ref, h1_ref, wu_ref, cw_ref, cb_ref, wd_ref, mod_ref, gf_ref,
         o_ref, ext_scr, up_scr, act_scr) = refs
    else:
        (ap_ref, ac_ref, an_ref, h1_ref, wu_ref, cw_ref, cb_ref, wd_ref, mod_ref,
         o_ref, ext_scr, up_scr, act_scr) = refs
    i = pl.program_id(1)
    nt = pl.num_programs(1)
    h0 = BF16_ROWS
    ext_scr[0:h0] = jnp.where(i > 0, ap_ref[0], jnp.zeros_like(ap_ref[0]))
    ext_scr[h0:h0 + tm] = ac_ref[0]
    ext_scr[h0 + tm:] = jnp.where(i < nt - 1, an_ref[0], jnp.zeros_like(an_ref[0]))
    n_chunks = wu_ref.shape[0]

    def up_proj(c, slot):
        up_scr[slot] = _dot(ext_scr[...], wu_ref[c])

    def conv_act(c, slot):
        up = up_scr[slot]
        cw = cw_ref[c]
        rows = tm + 2 * h0
        u = cb_ref[c] + cw[1:2] * up[h0:h0 + tm]
        u = u + cw[0:1] * pltpu.roll(up, 1, axis=0)[h0:h0 + tm]
        u = u + cw[2:3] * pltpu.roll(up, rows - 1, axis=0)[h0:h0 + tm]
        act_scr[c] = (_silu(u[:, :ck]) * u[:, ck:]).astype(BF16)

    up_proj(0, 0)
    n_pairs = (n_chunks - 1) // 2

    def pair(j, _):
        c = 2 * j
        up_proj(c + 1, 1)
        conv_act(c, 0)
        up_proj(c + 2, 0)
        conv_act(c + 1, 1)
        return 0

    for j in range(n_pairs):
        pair(j, 0)
    c = 2 * n_pairs
    if c + 1 < n_chunks:
        up_proj(c + 1, 1)
    conv_act(c, 0)
    if c + 1 < n_chunks:
        conv_act(c + 1, 1)

    acc = _dot(act_scr[0], wd_ref[0])
    for c in range(1, n_chunks):
        acc = acc + _dot(act_scr[c], wd_ref[c])
    h2 = h1_ref[0] + mod_ref[0, 5:6, :] * acc
    if final:
        h2 = _rms_rows(h2, gf_ref[...])
    o_ref[0] = h2


def _conv_ffn(a2, h1, wu, cw, cb, wd, mod, mod_row, final_g, tm):
    bsz, n, d = h1.shape
    n_chunks, _, ck2 = wu.shape
    ck = ck2 // 2
    hb = tm // BF16_ROWS
    nhb = n // BF16_ROWS
    final = final_g is not None
    tok = lambda b, i: (b, i, 0)
    prev = lambda b, i: (b, jnp.maximum(i * hb - 1, 0), 0)
    nxt = lambda b, i: (b, jnp.minimum((i + 1) * hb, nhb - 1), 0)
    const3 = lambda b, i: (0, 0, 0)
    single = pl.Buffered(1)
    in_specs = [pl.BlockSpec((1, BF16_ROWS, d), prev),
                pl.BlockSpec((1, tm, d), tok),
                pl.BlockSpec((1, BF16_ROWS, d), nxt),
                pl.BlockSpec((1, tm, d), tok),
                pl.BlockSpec(wu.shape, const3, pipeline_mode=single),
                pl.BlockSpec(cw.shape, const3),
                pl.BlockSpec(cb.shape, const3),
                pl.BlockSpec(wd.shape, const3, pipeline_mode=single),
                pl.BlockSpec((1, 6, d), lambda b, i: (mod_row(b), 0, 0))]
    args = [a2, a2, a2, h1, wu, cw, cb, wd, mod]
    if final:
        in_specs.append(pl.BlockSpec((1, d), lambda b, i: (0, 0)))
        args.append(final_g)
    return pl.pallas_call(
        functools.partial(_ffn_kernel, tm=tm, ck=ck, final=final),
        out_shape=jax.ShapeDtypeStruct((bsz, n, d), F32),
        grid=(bsz, n // tm),
        in_specs=in_specs,
        out_specs=pl.BlockSpec((1, tm, d), tok),
        scratch_shapes=[pltpu.VMEM((tm + 2 * BF16_ROWS, d), BF16),
                        pltpu.VMEM((2, tm + 2 * BF16_ROWS, ck2), F32),
                        pltpu.VMEM((n_chunks, tm, ck), BF16)],
        compiler_params=_cparams("parallel", "parallel"),
        name="conv_ffn_final" if final else "conv_ffn",
    )(*args)


def _rope_tables(n):
    pos = jnp.arange(n)
    row = (pos // GRID_W).astype(F32)
    col = (pos % GRID_W).astype(F32)
    n_freq = HEAD_DIM // 4
    inv = ROPE_BASE ** (-jnp.arange(n_freq, dtype=F32) / n_freq)
    ang = jnp.concatenate([row[:, None] * inv, col[:, None] * inv], axis=-1)
    cos, sin = jnp.cos(ang), jnp.sin(ang)
    cos_h = jnp.concatenate([cos, cos], axis=-1)
    sin_h = jnp.concatenate([-sin, sin], axis=-1)
    return jnp.tile(cos_h, (1, 4)), jnp.tile(sin_h, (1, 4))


def _block_diag(w):
    nb, c, d = w.shape
    eye = jnp.eye(nb, dtype=w.dtype)
    return jnp.einsum("ncd,nm->ncmd", w, eye).reshape(nb * c, nb * d)


def _ffn_chunks(d_ff):
    for ck in (256, 128):
        if d_ff % ck == 0:
            return ck
    raise ValueError("d_ff must be a multiple of 128")


def kernel(x, c, ctx, c_ctx, w_mod, b_mod, norm1_g, w_in, lru_conv_w, lru_conv_b, lru_w_a, lru_b_a, lru_w_x, lru_b_x, lru_lam, ga_q_norm_g, ga_k_norm_g, wa_sink, w_out, norm2_g, w_up, ffn_conv_w, ffn_conv_b, w_down, final_norm_g):
    bsz, n, d = x.shape
    n_ctx = ctx.shape[1]
    depth = w_in.shape[0]
    d_lru = lru_conv_w.shape[-1]
    d_ff = w_down.shape[1]
    assert w_in.shape[2] == 2 * d_lru + 1024 and n % 512 == 0 and n_ctx % 128 == 0 and bsz <= 16

    rows = 24
    cvec = jnp.zeros((rows, d), F32).at[:bsz].set(c).at[bsz].set(c_ctx)
    mod_all = _modulation(cvec, w_mod, b_mod).reshape(depth, rows, 6, d)
    lat_row = lambda b: b
    ctx_row = lambda b: bsz

    cos, sin = _rope_tables(n)
    seg = _block_diag(jnp.ones((4, HEAD_DIM, HEAD_DIM), BF16))
    ck = _ffn_chunks(d_ff)
    n_ck = d_ff // ck

    tm_lat, tm_ctx = 512, n_ctx
    h, hc = x, ctx
    for l in range(depth):
        need_ctx = l < depth - 1
        mod = mod_all[l]
        g1 = norm1_g[l].reshape(1, d)
        g2 = norm2_g[l].reshape(1, d)
        w_in_l = w_in[l].astype(BF16)
        qg = jnp.tile(ga_q_norm_g[l], 4).reshape(1, 256)
        kg = jnp.tile(ga_k_norm_g[l], 2).reshape(1, 128)
        sink = wa_sink[l]
        conv_w = lru_conv_w[l]
        conv_b = lru_conv_b[l].reshape(1, d_lru)
        w_out_l = w_out[l].astype(BF16)
        wu = w_up[l].astype(BF16)
        wu = jnp.concatenate([wu[:, :d_ff].reshape(d, n_ck, ck), wu[:, d_ff:].reshape(d, n_ck, ck)], axis=-1)
        wu = wu.transpose(1, 0, 2)
        fcw = ffn_conv_w[l]
        fcw = jnp.concatenate([fcw[:, :d_ff].reshape(-1, n_ck, ck), fcw[:, d_ff:].reshape(-1, n_ck, ck)], axis=-1)
        fcw = fcw.transpose(1, 0, 2)
        fcb = ffn_conv_b[l]
        fcb = jnp.concatenate([fcb[:d_ff].reshape(n_ck, 1, ck), fcb[d_ff:].reshape(n_ck, 1, ck)], axis=-1)
        wd = w_down[l].astype(BF16).reshape(n_ck, ck, d)

        wg = [(0.5 * jnp.concatenate([_block_diag(lru_w_a[l, dr]), _block_diag(lru_w_x[l, dr])], axis=1)).astype(BF16)
              for dr in range(2)]
        bg = [0.5 * jnp.concatenate([lru_b_a[l, dr], lru_b_x[l, dr]]).reshape(1, 2 * d_lru) for dr in range(2)]
        lam = [lru_lam[l, dr].reshape(1, d_lru) for dr in range(2)]
        zero_state = jnp.zeros((bsz, 1, d_lru), F32)

        u_c, gate_c, hf_c, last_f, gq_c, gk_c, gv_c, wq_c, wk_c, wv_c = _in_projection(
            hc, mod, ctx_row, g1, w_in_l, seg, qg, kg, None, None,
            conv_w, conv_b, wg[0], bg[0], lam[0], zero_state, tm_ctx)
        u_l, gate_l, hf_l, _, gq_l, gk_l, gv_l, wq_l, wk_l, wv_l = _in_projection(
            h, mod, lat_row, g1, w_in_l, seg, qg, kg, cos, sin,
            conv_w, conv_b, wg[0], bg[0], lam[0], last_f, tm_lat)

        yb_l = _global_attention(gq_l, gk_l, gv_l, gk_c, gv_c, 512, 512)
        yc_l = _window_attention(sink, wq_l, wk_l, wv_l, wk_c, wv_c, 4)

        if need_ctx:
            yb_c, yc_c = _context_attention(sink, gq_c, gk_c, gv_c, wq_c, wk_c, wv_c)
            hc1, ac2, last_b = _lru_backward(u_c, wg[1], bg[1], lam[1], zero_state, n_ctx,
                                             proj=(hf_c, gate_c, hc, yb_c, yc_c, w_out_l, mod, ctx_row, g2))
        else:
            (last_b,) = _lru_backward(u_c, wg[1], bg[1], lam[1], zero_state, n_ctx)
        h1, a2, _ = _lru_backward(u_l, wg[1], bg[1], lam[1], last_b, 512,
                                  proj=(hf_l, gate_l, h, yb_l, yc_l, w_out_l, mod, lat_row, g2))

        final_g = None if need_ctx else final_norm_g.reshape(1, d)
        h = _conv_ffn(a2, h1, wu, fcw, fcb, wd, mod, lat_row, final_g, tm_lat)
        if need_ctx:
            hc = _conv_ffn(ac2, hc1, wu, fcw, fcb, wd, mod, ctx_row, None, tm_ctx)
    return h
```

```python
import functools

import jax
import jax.numpy as jnp
from jax import lax
from jax.experimental import pallas as pl
from jax.experimental.pallas import tpu as pltpu

F32 = jnp.float32
BF16 = jnp.bfloat16

HEAD_DIM = 64
GRID_W = 64
WINDOW = 128
Q_BLOCK = 128
LRU_C = 8.0
ROPE_BASE = 10000.0
EPS = 1e-6
NEG_INF = -1e30
ATTN_SCALE = HEAD_DIM ** -0.5
LOG2E = 1.4426950408889634
Q_SCALE = ATTN_SCALE * LOG2E
LANES = 128
SUBLANES = 8
BF16_ROWS = 16
VMEM_LIMIT = 56 * 1024 * 1024

TOKEN_TILE = 512
ATTN_Q_TILE = 512
ATTN_K_CHUNK = 1024
WINDOW_BLOCKS = 4
MOD_COL_TILE = 1536


def _cparams(*sem):
    return pltpu.CompilerParams(dimension_semantics=sem, vmem_limit_bytes=VMEM_LIMIT)


def _rms_rows(x, g):
    ms = jnp.mean(x * x, axis=-1, keepdims=True)
    return x * lax.rsqrt(ms + EPS) * g


def _ada_norm(x, g, shift, scale):
    ms = jnp.mean(x * x, axis=-1, keepdims=True)
    return (x * lax.rsqrt(ms + EPS)) * (g * (1.0 + scale)) + shift


def _silu(x):
    half = 0.5 * x
    return half * (1.0 + jnp.tanh(half))


def _gelu_tanh(x):
    c = 0.7978845608028654
    return 0.5 * x * (1.0 + jnp.tanh(c * (x + 0.044715 * (x * x * x))))


def _dot(a, b):
    return jnp.dot(a, b, preferred_element_type=F32)


def _dot_nt(a, b):
    return lax.dot_general(a, b, (((1,), (1,)), ((), ())), preferred_element_type=F32)


def _mod_kernel(c_ref, w_ref, b_ref, o_ref):
    s = _silu(c_ref[...])
    o_ref[0] = jnp.dot(s, w_ref[0], preferred_element_type=F32,
                       precision=lax.Precision.HIGHEST) + b_ref[0]


def _modulation(cvec, w_mod, b_mod):
    n_layers, d, d6 = w_mod.shape
    rows = cvec.shape[0]
    tn = MOD_COL_TILE
    return pl.pallas_call(
        _mod_kernel,
        out_shape=jax.ShapeDtypeStruct((n_layers, rows, d6), F32),
        grid=(n_layers, d6 // tn),
        in_specs=[pl.BlockSpec((rows, d), lambda l, j: (0, 0)),
                  pl.BlockSpec((1, d, tn), lambda l, j: (l, 0, j)),
                  pl.BlockSpec((1, 1, tn), lambda l, j: (l, 0, j))],
        out_specs=pl.BlockSpec((1, rows, tn), lambda l, j: (l, 0, j)),
        compiler_params=_cparams("parallel", "parallel"),
        name="modulation",
    )(cvec, w_mod, b_mod.reshape(n_layers, 1, d6))


def _seg_mean_sq(x, seg):
    sq = x * x
    hi = sq.astype(BF16)
    lo = (sq - hi.astype(F32)).astype(BF16)
    return (_dot(hi, seg) + _dot(lo, seg)) * (1.0 / HEAD_DIM)


def _swap_halves(x):
    w = x.shape[-1]
    lane = lax.broadcasted_iota(jnp.int32, x.shape, 1)
    first = (lane % HEAD_DIM) < (HEAD_DIM // 2)
    up = pltpu.roll(x, w - HEAD_DIM // 2, axis=1)
    down = pltpu.roll(x, HEAD_DIM // 2, axis=1)
    return jnp.where(first, up, down)


def _dup_kv_heads(k):
    lane = lax.broadcasted_iota(jnp.int32, k.shape, 1)
    lo = lane < HEAD_DIM
    sw = pltpu.roll(k, HEAD_DIM, axis=1)
    return jnp.concatenate([jnp.where(lo, k, sw), jnp.where(lo, sw, k)], axis=1)


def _aug_kv_heads(v):
    lane = lax.broadcasted_iota(jnp.int32, v.shape, 1)
    lo = lane < HEAD_DIM
    sw = pltpu.roll(v, HEAD_DIM, axis=1)
    return jnp.concatenate([jnp.where(lo, v, 1.0), jnp.where(lo, sw, 1.0)], axis=1)


def _sqrt_nonneg(x):
    return x * lax.rsqrt(jnp.maximum(x, 1e-30))


def _lru_gates(u, wg_ref, bg_ref, lam_ref, a_scr, d_scr):
    d_lru = u.shape[-1]
    th = jnp.tanh(_dot(u.astype(BF16), wg_ref[...]) + bg_ref[...])
    nlam = -lam_ref[...]
    sp = jnp.maximum(nlam, 0.0) + jnp.log1p(jnp.exp(-jnp.abs(nlam)))
    half_c = (-0.5 * LRU_C * LOG2E) * sp
    a = jnp.exp2(half_c * th[:, :d_lru] + half_c)
    i_g = 0.5 * th[:, d_lru:] + 0.5
    a_scr[...] = a
    d_scr[...] = _sqrt_nonneg(1.0 - a * a) * i_g * u


def _lru_scan(a_scr, d_scr, carry_scr, reverse, unroll=2):
    tl, d_lru = a_scr.shape
    row = lax.broadcasted_iota(jnp.int32, (SUBLANES, d_lru), 0)
    n_chunks = tl // SUBLANES

    def chunk(c, carry):
        cc = (n_chunks - 1 - c) if reverse else c
        off = pl.multiple_of(cc * SUBLANES, SUBLANES)
        av = a_scr[pl.ds(off, SUBLANES), :]
        dv = d_scr[pl.ds(off, SUBLANES), :]
        for s in (1, 2, 4):
            if reverse:
                a_sh = pltpu.roll(av, SUBLANES - s, axis=0)
                d_sh = pltpu.roll(dv, SUBLANES - s, axis=0)
                ok = row < SUBLANES - s
            else:
                a_sh = pltpu.roll(av, s, axis=0)
                d_sh = pltpu.roll(dv, s, axis=0)
                ok = row >= s
            dv = jnp.where(ok, av * d_sh + dv, dv)
            av = jnp.where(ok, av * a_sh, av)
        hv = av * carry + dv
        d_scr[pl.ds(off, SUBLANES), :] = hv
        edge = hv[0:1, :] if reverse else hv[SUBLANES - 1:SUBLANES, :]
        return jnp.broadcast_to(edge, (SUBLANES, d_lru))

    carry = lax.fori_loop(0, n_chunks, chunk, carry_scr[...], unroll=unroll)
    carry_scr[...] = carry
    return carry


def _inproj_kernel(*refs, rope, d_lru, tm):
    if rope:
        (h_ref, hn_ref, mod_ref, g_ref, w_ref, seg_ref, qg_ref, kg_ref, cos_ref, sin_ref,
         cw_ref, cb_ref, wg_ref, bg_ref, lam_ref, h0_ref,
         u_ref, gate_ref, hf_ref, last_ref, gq_ref, gk_ref, gv_ref, wq_ref, wk_ref, wv_ref,
         tail_scr, a_scr, d_scr, carry_scr) = refs
    else:
        (h_ref, hn_ref, mod_ref, g_ref, w_ref, seg_ref, qg_ref, kg_ref,
         cw_ref, cb_ref, wg_ref, bg_ref, lam_ref, h0_ref,
         u_ref, gate_ref, hf_ref, last_ref, gq_ref, gk_ref, gv_ref, wq_ref, wk_ref, wv_ref,
         tail_scr, a_scr, d_scr, carry_scr) = refs
    i = pl.program_id(1)
    nt = pl.num_programs(1)

    @pl.when(i == 0)
    def _():
        carry_scr[...] = jnp.broadcast_to(h0_ref[0], carry_scr.shape)
        tail_scr[...] = jnp.zeros_like(tail_scr)

    x = jnp.concatenate([h_ref[0], hn_ref[0]], axis=0)
    a = _ada_norm(x, g_ref[...], mod_ref[0, 0:1, :], mod_ref[0, 1:2, :])
    y_ext = _dot(a.astype(BF16), w_ref[...])
    y = y_ext[:tm]
    o = 0
    r = y[:, o:o + d_lru]; o += d_lru
    gate_ref[0] = y[:, o:o + d_lru]; o += d_lru
    gq = y[:, o:o + 256]; o += 256
    gk = y[:, o:o + 128]; o += 128
    gv = y[:, o:o + 128]; o += 128
    wq = y[:, o:o + 256]; o += 256
    wk = y[:, o:o + 128]; o += 128
    wv = y[:, o:o + 128]

    seg = seg_ref[...]
    gq = gq * lax.rsqrt(_seg_mean_sq(gq, seg) + EPS) * qg_ref[...]
    gk = gk * lax.rsqrt(_seg_mean_sq(gk, seg[:128, :128]) + EPS) * kg_ref[...]
    if rope:
        cos = cos_ref[...]
        sin = sin_ref[...]
        gq = gq * cos + _swap_halves(gq) * sin
        wq = wq * cos + _swap_halves(wq) * sin
        gk = gk * cos[:, :128] + _swap_halves(gk) * sin[:, :128]
        wk = wk * cos[:, :128] + _swap_halves(wk) * sin[:, :128]
    gq_ref[0] = (gq * Q_SCALE).astype(BF16)
    wq_ref[0] = (wq * Q_SCALE).astype(BF16)
    gk_ref[0] = _dup_kv_heads(gk).astype(BF16)
    gv_ref[0] = _aug_kv_heads(gv).astype(BF16)
    wk_ref[0] = _dup_kv_heads(wk).astype(BF16)
    wv_ref[0] = _aug_kv_heads(wv).astype(BF16)

    has_next = (i < nt - 1).astype(F32)
    ext = jnp.concatenate([tail_scr[...], r, y_ext[tm:, :d_lru] * has_next], axis=0)
    tail_scr[...] = r[tm - SUBLANES:]
    rows = tm + 2 * SUBLANES
    lo, hi = SUBLANES, SUBLANES + tm
    cw = cw_ref[...]
    u = cb_ref[...] + cw[2:3] * ext[lo:hi]
    u = u + cw[0:1] * pltpu.roll(ext, 2, axis=0)[lo:hi]
    u = u + cw[1:2] * pltpu.roll(ext, 1, axis=0)[lo:hi]
    u = u + cw[3:4] * pltpu.roll(ext, rows - 1, axis=0)[lo:hi]
    u_ref[0] = u

    _lru_gates(u, wg_ref, bg_ref, lam_ref, a_scr, d_scr)
    carry = _lru_scan(a_scr, d_scr, carry_scr, False, unroll=True)
    hf_ref[0] = d_scr[...]
    last_ref[0] = carry[0:1, :]


def _in_projection(h, mod, mod_row, g1, w_in, seg, qg, kg, cos, sin, conv_w, conv_b, wg, bg, lam, h0, tm):
    bsz, n, d = h.shape
    d_in = w_in.shape[1]
    d_lru = conv_w.shape[-1]
    rope = cos is not None
    nhb = n // SUBLANES
    tok = lambda b, i: (b, i, 0)
    nxt = lambda b, i: (b, jnp.minimum((i + 1) * (tm // SUBLANES), nhb - 1), 0)
    const2 = lambda b, i: (0, 0)
    state_spec = pl.BlockSpec((1, 1, d_lru), lambda b, i: (b, 0, 0))
    in_specs = [pl.BlockSpec((1, tm, d), tok),
                pl.BlockSpec((1, SUBLANES, d), nxt),
                pl.BlockSpec((1, 6, d), lambda b, i: (mod_row(b), 0, 0)),
                pl.BlockSpec((1, d), const2),
                pl.BlockSpec((d, d_in), const2),
                pl.BlockSpec((256, 256), const2),
                pl.BlockSpec((1, 256), const2),
                pl.BlockSpec((1, 128), const2)]
    args = [h, h, mod, g1, w_in, seg, qg, kg]
    if rope:
        in_specs += [pl.BlockSpec((tm, 256), lambda b, i: (i, 0)),
                     pl.BlockSpec((tm, 256), lambda b, i: (i, 0))]
        args += [cos, sin]
    in_specs += [pl.BlockSpec(conv_w.shape, const2),
                 pl.BlockSpec((1, d_lru), const2),
                 pl.BlockSpec(wg.shape, const2),
                 pl.BlockSpec((1, 2 * d_lru), const2),
                 pl.BlockSpec((1, d_lru), const2),
                 state_spec]
    args += [conv_w, conv_b, wg, bg, lam, h0]
    out_shape = ([jax.ShapeDtypeStruct((bsz, n, d_lru), F32)] * 3
                 + [jax.ShapeDtypeStruct((bsz, 1, d_lru), F32)]
                 + [jax.ShapeDtypeStruct((bsz, n, 256), BF16)] * 6)
    out_specs = ([pl.BlockSpec((1, tm, d_lru), tok)] * 3 + [state_spec]
                 + [pl.BlockSpec((1, tm, 256), tok)] * 6)
    return pl.pallas_call(
        functools.partial(_inproj_kernel, rope=rope, d_lru=d_lru, tm=tm),
        out_shape=out_shape,
        grid=(bsz, n // tm),
        in_specs=in_specs,
        out_specs=out_specs,
        scratch_shapes=[pltpu.VMEM((SUBLANES, d_lru), F32), pltpu.VMEM((tm, d_lru), F32),
                        pltpu.VMEM((tm, d_lru), F32), pltpu.VMEM((SUBLANES, d_lru), F32)],
        compiler_params=_cparams("parallel", "arbitrary"),
        name="in_projection_lat" if rope else "in_projection_ctx",
    )(*args)


def _lru_bwd_kernel(*refs, project):
    if project:
        (u_ref, wg_ref, bg_ref, lam_ref, h0_ref, hf_ref, gate_ref,
         h_ref, yb_ref, yc_ref, wa_ref, wb_ref, wc_ref, mod_ref, g2_ref,
         h1_ref, a2_ref, last_ref, a_scr, d_scr, carry_scr) = refs
    else:
        (u_ref, wg_ref, bg_ref, lam_ref, h0_ref, last_ref, a_scr, d_scr, carry_scr) = refs
    j = pl.program_id(1)

    @pl.when(j == 0)
    def _():
        carry_scr[...] = jnp.broadcast_to(h0_ref[0], carry_scr.shape)

    _lru_gates(u_ref[0], wg_ref, bg_ref, lam_ref, a_scr, d_scr)
    carry = _lru_scan(a_scr, d_scr, carry_scr, True, unroll=True)
    last_ref[0] = carry[0:1, :]
    if not project:
        return
    ya = ((d_scr[...] + hf_ref[0]) * _gelu_tanh(gate_ref[0])).astype(BF16)
    proj = _dot(ya, wa_ref[...]) + _dot(yb_ref[0], wb_ref[...]) + _dot(yc_ref[0], wc_ref[...])
    h1 = h_ref[0] + mod_ref[0, 2:3, :] * proj
    h1_ref[0] = h1
    a2 = _ada_norm(h1, g2_ref[...], mod_ref[0, 3:4, :], mod_ref[0, 4:5, :])
    a2_ref[0] = a2.astype(a2_ref.dtype)


def _lru_backward(u, wg, bg, lam, h0, tl, proj=None):
    bsz, n, c = u.shape
    nt = n // tl
    cur = lambda b, j: (b, nt - 1 - j, 0)
    const2 = lambda b, j: (0, 0)
    state_spec = pl.BlockSpec((1, 1, c), lambda b, j: (b, 0, 0))
    in_specs = [pl.BlockSpec((1, tl, c), cur),
                pl.BlockSpec(wg.shape, const2),
                pl.BlockSpec((1, 2 * c), const2),
                pl.BlockSpec((1, c), const2),
                state_spec]
    args = [u, wg, bg, lam, h0]
    out_shape, out_specs = [], []
    if proj is not None:
        hf, gate, h, yb, yc, w_out, mod, mod_row, g2 = proj
        d = h.shape[-1]
        cb_, cc_ = yb.shape[-1], yc.shape[-1]
        in_specs += [pl.BlockSpec((1, tl, c), cur), pl.BlockSpec((1, tl, c), cur),
                     pl.BlockSpec((1, tl, d), cur),
                     pl.BlockSpec((1, tl, cb_), cur), pl.BlockSpec((1, tl, cc_), cur),
                     pl.BlockSpec((c, d), const2), pl.BlockSpec((cb_, d), const2), pl.BlockSpec((cc_, d), const2),
                     pl.BlockSpec((1, 6, d), lambda b, j: (mod_row(b), 0, 0)),
                     pl.BlockSpec((1, d), const2)]
        args += [hf, gate, h, yb, yc, w_out[:c], w_out[c:c + cb_], w_out[c + cb_:], mod, g2]
        out_shape = [jax.ShapeDtypeStruct((bsz, n, d), F32), jax.ShapeDtypeStruct((bsz, n, d), BF16)]
        out_specs = [pl.BlockSpec((1, tl, d), cur), pl.BlockSpec((1, tl, d), cur)]
    return pl.pallas_call(
        functools.partial(_lru_bwd_kernel, project=proj is not None),
        out_shape=out_shape + [jax.ShapeDtypeStruct((bsz, 1, c), F32)],
        grid=(bsz, nt),
        in_specs=in_specs,
        out_specs=out_specs + [state_spec],
        scratch_shapes=[pltpu.VMEM((tl, c), F32), pltpu.VMEM((tl, c), F32),
                        pltpu.VMEM((SUBLANES, c), F32)],
        compiler_params=_cparams("parallel", "arbitrary"),
        name="lru_bwd_project" if proj is not None else "lru_bwd_state",
    )(*args)


def _stack_heads(qg):
    lane = lax.broadcasted_iota(jnp.int32, qg.shape, 1)
    lo = lane < HEAD_DIM
    zero = jnp.zeros_like(qg)
    return jnp.concatenate([jnp.where(lo, qg, zero), jnp.where(lo, zero, qg)], axis=0)


def _normalise_heads(acc, extra_den, tq):
    den = pltpu.roll(acc, HEAD_DIM, axis=1)
    if extra_den is not None:
        den = den + extra_den
    res = acc / den
    lane = lax.broadcasted_iota(jnp.int32, (tq, LANES), 1)
    return jnp.where(lane < HEAD_DIM, res[:tq], pltpu.roll(res[tq:], HEAD_DIM, axis=1))


def _gattn_kernel(q_ref, kl_ref, vl_ref, kc_ref, vc_ref, o_ref, *, tk):
    tq = q_ref.shape[1]
    n_lat = kl_ref.shape[1]
    n_ctx = kc_ref.shape[1]
    chunks = [(kl_ref, vl_ref, c * tk, tk) for c in range(n_lat // tk)] + [(kc_ref, vc_ref, 0, n_ctx)]
    for g in range(2):
        gs = slice(LANES * g, LANES * (g + 1))
        q2 = _stack_heads(q_ref[0, :, gs])
        m = jnp.full((2 * tq, 1), NEG_INF, F32)
        acc = jnp.zeros((2 * tq, LANES), F32)
        for k_ref, v_ref, off, size in chunks:
            s = _dot_nt(q2, k_ref[0, off:off + size, gs])
            m_new = jnp.maximum(m, jnp.max(s, axis=1, keepdims=True))
            p = jnp.exp2(s - m_new)
            acc = jnp.exp2(m - m_new) * acc + _dot(p.astype(BF16), v_ref[0, off:off + size, gs])
            m = m_new
        o_ref[0, :, gs] = _normalise_heads(acc, None, tq).astype(o_ref.dtype)


def _global_attention(q, k_lat, v_lat, k_ctx, v_ctx, tq, tk):
    bsz, n, _ = q.shape
    n_ctx = k_ctx.shape[1]
    full = lambda b, i: (b, 0, 0)
    return pl.pallas_call(
        functools.partial(_gattn_kernel, tk=tk),
        out_shape=jax.ShapeDtypeStruct((bsz, n, 256), BF16),
        grid=(bsz, n // tq),
        in_specs=[pl.BlockSpec((1, tq, 256), lambda b, i: (b, i, 0)),
                  pl.BlockSpec((1, n, 256), full), pl.BlockSpec((1, n, 256), full),
                  pl.BlockSpec((1, n_ctx, 256), full), pl.BlockSpec((1, n_ctx, 256), full)],
        out_specs=pl.BlockSpec((1, tq, 256), lambda b, i: (b, i, 0)),
        compiler_params=_cparams("parallel", "arbitrary"),
        name="global_attention",
    )(q, k_lat, v_lat, k_ctx, v_ctx)


def _sink_column(sink_ref, g, rows):
    row = lax.broadcasted_iota(jnp.int32, (2 * rows, 1), 0)
    return jnp.where(row < rows, sink_ref[2 * g], sink_ref[2 * g + 1]) * LOG2E


def _wattn_kernel(sink_ref, q_ref, kp_ref, kc_ref, kn_ref, vp_ref, vc_ref, vn_ref, kx_ref, vx_ref, o_ref,
                  kcat, vcat, *, nsub):
    i = pl.program_id(1)
    nb = pl.num_programs(1)
    w = WINDOW
    tq = nsub * w
    kcat[0:w] = kp_ref[0]
    kcat[w:w + tq] = kc_ref[0]
    kcat[w + tq:] = kn_ref[0]
    vcat[0:w] = vp_ref[0]
    vcat[w:w + tq] = vc_ref[0]
    vcat[w + tq:] = vn_ref[0]
    pw = 2 * w
    npair = nsub // 2
    rq = lax.broadcasted_iota(jnp.int32, (2 * pw, 2 * pw), 0) % pw
    col = lax.broadcasted_iota(jnp.int32, (2 * pw, 2 * pw), 1)
    band = (col >= rq) & (col <= rq + 2 * w)
    for g in range(2):
        gs = slice(LANES * g, LANES * (g + 1))
        q2_all = _stack_heads(q_ref[0, :, gs])
        s_ctx_all = _dot_nt(q2_all, kx_ref[0, :, gs])
        for p in range(npair):
            valid = band
            if p == 0:
                valid = valid & ((col >= w) | (i > 0))
            if p == npair - 1:
                valid = valid & ((col < 3 * w) | (i < nb - 1))
            r0 = slice(p * pw, (p + 1) * pw)
            r1 = slice(tq + p * pw, tq + (p + 1) * pw)
            keys = slice(p * pw, p * pw + 2 * pw)
            q2 = jnp.concatenate([q2_all[r0], q2_all[r1]], axis=0)
            s_loc = jnp.where(valid, _dot_nt(q2, kcat[keys, gs]), NEG_INF)
            s_ctx = jnp.concatenate([s_ctx_all[r0], s_ctx_all[r1]], axis=0)
            s_sink = _sink_column(sink_ref, g, pw)
            m = jnp.maximum(jnp.maximum(jnp.max(s_loc, axis=1, keepdims=True),
                                        jnp.max(s_ctx, axis=1, keepdims=True)), s_sink)
            p_loc = jnp.exp2(s_loc - m)
            p_ctx = jnp.exp2(s_ctx - m)
            acc = _dot(p_loc.astype(BF16), vcat[keys, gs]) + _dot(p_ctx.astype(BF16), vx_ref[0, :, gs])
            o_ref[0, r0, gs] = _normalise_heads(acc, jnp.exp2(s_sink - m), pw).astype(o_ref.dtype)


def _window_attention(sink, q, k, v, k_ctx, v_ctx, nsub):
    bsz, n, _ = q.shape
    n_ctx = k_ctx.shape[1]
    tq = nsub * Q_BLOCK
    nb128 = n // Q_BLOCK
    cur = lambda b, i: (b, i, 0)
    prev = lambda b, i: (b, jnp.maximum(i * nsub - 1, 0), 0)
    nxt = lambda b, i: (b, jnp.minimum((i + 1) * nsub, nb128 - 1), 0)
    full = lambda b, i: (b, 0, 0)
    halo = lambda im: pl.BlockSpec((1, Q_BLOCK, 256), im)
    big = pl.BlockSpec((1, tq, 256), cur)
    return pl.pallas_call(
        functools.partial(_wattn_kernel, nsub=nsub),
        out_shape=jax.ShapeDtypeStruct((bsz, n, 256), BF16),
        grid=(bsz, n // tq),
        in_specs=[pl.BlockSpec(memory_space=pltpu.SMEM),
                  big, halo(prev), big, halo(nxt), halo(prev), big, halo(nxt),
                  pl.BlockSpec((1, n_ctx, 256), full), pl.BlockSpec((1, n_ctx, 256), full)],
        out_specs=big,
        scratch_shapes=[pltpu.VMEM((tq + 2 * Q_BLOCK, 256), BF16), pltpu.VMEM((tq + 2 * Q_BLOCK, 256), BF16)],
        compiler_params=_cparams("parallel", "arbitrary"),
        name="window_attention",
    )(sink, q, k, k, k, v, v, v, k_ctx, v_ctx)


def _cattn_kernel(sink_ref, gq_ref, gk_ref, gv_ref, wq_ref, wk_ref, wv_ref, yb_ref, yc_ref):
    tq = gq_ref.shape[1]
    for g in range(2):
        gs = slice(LANES * g, LANES * (g + 1))
        s = _dot_nt(_stack_heads(gq_ref[0, :, gs]), gk_ref[0, :, gs])
        p = jnp.exp2(s - jnp.max(s, axis=1, keepdims=True))
        acc = _dot(p.astype(BF16), gv_ref[0, :, gs])
        yb_ref[0, :, gs] = _normalise_heads(acc, None, tq).astype(yb_ref.dtype)
        s = _dot_nt(_stack_heads(wq_ref[0, :, gs]), wk_ref[0, :, gs])
        s_sink = _sink_column(sink_ref, g, tq)
        m = jnp.maximum(jnp.max(s, axis=1, keepdims=True), s_sink)
        p = jnp.exp2(s - m)
        acc = _dot(p.astype(BF16), wv_ref[0, :, gs])
        yc_ref[0, :, gs] = _normalise_heads(acc, jnp.exp2(s_sink - m), tq).astype(yc_ref.dtype)


def _context_attention(sink, gq, gk, gv, wq, wk, wv):
    bsz, n_ctx, _ = gq.shape
    blk = pl.BlockSpec((1, n_ctx, 256), lambda b: (b, 0, 0))
    return pl.pallas_call(
        _cattn_kernel,
        out_shape=[jax.ShapeDtypeStruct((bsz, n_ctx, 256), BF16)] * 2,
        grid=(bsz,),
        in_specs=[pl.BlockSpec(memory_space=pltpu.SMEM)] + [blk] * 6,
        out_specs=[blk, blk],
        compiler_params=_cparams("parallel"),
        name="context_attention",
    )(sink, gq, gk, gv, wq, wk, wv)


def _ffn_kernel(*refs, tm, ck, final):
    if final:
        (ap_ref, ac_ref, an_ref, h1_ref, wu_ref, cw_ref, cb_ref, wd_ref, mod_ref, gf_ref,
         o_ref, ext_scr, up_scr, act_scr) = refs
    else:
        (ap_ref, ac_ref, an_ref, h1_ref, wu_ref, cw_ref, cb_ref, wd_ref, mod_ref,
         o_ref, ext_scr, up_scr, act_scr) = refs
    i = pl.program_id(1)
    nt = pl.num_programs(1)
    h0 = BF16_ROWS
    ext_scr[0:h0] = jnp.where(i > 0, ap_ref[0], jnp.zeros_like(ap_ref[0]))
    ext_scr[h0:h0 + tm] = ac_ref[0]
    ext_scr[h0 + tm:] = jnp.where(i < nt - 1, an_ref[0], jnp.zeros_like(an_ref[0]))
    n_chunks = wu_ref.shape[0]

    def up_proj(c, slot):
        up_scr[slot] = _dot(ext_scr[...], wu_ref[c])

    def conv_act(c, slot):
        up = up_scr[slot]
        cw = cw_ref[c]
        rows = tm + 2 * h0
        u = cb_ref[c] + cw[1:2] * up[h0:h0 + tm]
        u = u + cw[0:1] * pltpu.roll(up, 1, axis=0)[h0:h0 + tm]
        u = u + cw[2:3] * pltpu.roll(up, rows - 1, axis=0)[h0:h0 + tm]
        act_scr[c] = (_silu(u[:, :ck]) * u[:, ck:]).astype(BF16)

    up_proj(0, 0)
    n_pairs = (n_chunks - 1) // 2

    def pair(j, _):
        c = 2 * j
        up_proj(c + 1, 1)
        conv_act(c, 0)
        up_proj(c + 2, 0)
        conv_act(c + 1, 1)
        return 0

    for j in range(n_pairs):
        pair(j, 0)
    c = 2 * n_pairs
    if c + 1 < n_chunks:
        up_proj(c + 1, 1)
    conv_act(c, 0)
    if c + 1 < n_chunks:
        conv_act(c + 1, 1)

    acc = _dot(act_scr[0], wd_ref[0])
    for c in range(1, n_chunks):
        acc = acc + _dot(act_scr[c], wd_ref[c])
    h2 = h1_ref[0] + mod_ref[0, 5:6, :] * acc
    if final:
        h2 = _rms_rows(h2, gf_ref[...])
    o_ref[0] = h2


def _conv_ffn(a2, h1, wu, cw, cb, wd, mod, mod_row, final_g, tm):
    bsz, n, d = h1.shape
    n_chunks, _, ck2 = wu.shape
    ck = ck2 // 2
    hb = tm // BF16_ROWS
    nhb = n // BF16_ROWS
    final = final_g is not None
    tok = lambda b, i: (b, i, 0)
    prev = lambda b, i: (b, jnp.maximum(i * hb - 1, 0), 0)
    nxt = lambda b, i: (b, jnp.minimum((i + 1) * hb, nhb - 1), 0)
    const3 = lambda b, i: (0, 0, 0)
    single = pl.Buffered(1)
    in_specs = [pl.BlockSpec((1, BF16_ROWS, d), prev),
                pl.BlockSpec((1, tm, d), tok),
                pl.BlockSpec((1, BF16_ROWS, d), nxt),
                pl.BlockSpec((1, tm, d), tok),
                pl.BlockSpec(wu.shape, const3, pipeline_mode=single),
                pl.BlockSpec(cw.shape, const3),
                pl.BlockSpec(cb.shape, const3),
                pl.BlockSpec(wd.shape, const3, pipeline_mode=single),
                pl.BlockSpec((1, 6, d), lambda b, i: (mod_row(b), 0, 0))]
    args = [a2, a2, a2, h1, wu, cw, cb, wd, mod]
    if final:
        in_specs.append(pl.BlockSpec((1, d), lambda b, i: (0, 0)))
        args.append(final_g)
    return pl.pallas_call(
        functools.partial(_ffn_kernel, tm=tm, ck=ck, final=final),
        out_shape=jax.ShapeDtypeStruct((bsz, n, d), F32),
        grid=(bsz, n // tm),
        in_specs=in_specs,
        out_specs=pl.BlockSpec((1, tm, d), tok),
        scratch_shapes=[pltpu.VMEM((tm + 2 * BF16_ROWS, d), BF16),
                        pltpu.VMEM((2, tm + 2 * BF16_ROWS, ck2), F32),
                        pltpu.VMEM((n_chunks, tm, ck), BF16)],
        compiler_params=_cparams("parallel", "parallel"),
        name="conv_ffn_final" if final else "conv_ffn",
    )(*args)


def _rope_tables(n):
    pos = jnp.arange(n)
    row = (pos // GRID_W).astype(F32)
    col = (pos % GRID_W).astype(F32)
    n_freq = HEAD_DIM // 4
    inv = ROPE_BASE ** (-jnp.arange(n_freq, dtype=F32) / n_freq)
    ang = jnp.concatenate([row[:, None] * inv, col[:, None] * inv], axis=-1)
    cos, sin = jnp.cos(ang), jnp.sin(ang)
    cos_h = jnp.concatenate([cos, cos], axis=-1)
    sin_h = jnp.concatenate([-sin, sin], axis=-1)
    return jnp.tile(cos_h, (1, 4)), jnp.tile(sin_h, (1, 4))


def _block_diag(w):
    nb, c, d = w.shape
    eye = jnp.eye(nb, dtype=w.dtype)
    return jnp.einsum("ncd,nm->ncmd", w, eye).reshape(nb * c, nb * d)


def _ffn_chunks(d_ff):
    for ck in (256, 128):
        if d_ff % ck == 0:
            return ck
    raise ValueError("d_ff must be a multiple of 128")


def kernel(x, c, ctx, c_ctx, w_mod, b_mod, norm1_g, w_in, lru_conv_w, lru_conv_b, lru_w_a, lru_b_a, lru_w_x, lru_b_x, lru_lam, ga_q_norm_g, ga_k_norm_g, wa_sink, w_out, norm2_g, w_up, ffn_conv_w, ffn_conv_b, w_down, final_norm_g):
    bsz, n, d = x.shape
    n_ctx = ctx.shape[1]
    depth = w_in.shape[0]
    d_lru = lru_conv_w.shape[-1]
    d_ff = w_down.shape[1]
    assert w_in.shape[2] == 2 * d_lru + 1024 and n % ATTN_K_CHUNK == 0 and n_ctx % 128 == 0 and bsz <= 16

    rows = 24
    cvec = jnp.zeros((rows, d), F32).at[:bsz].set(c).at[bsz].set(c_ctx)
    mod_all = _modulation(cvec, w_mod, b_mod).reshape(depth, rows, 6, d)
    lat_row = lambda b: b
    ctx_row = lambda b: bsz

    cos, sin = _rope_tables(n)
    seg = _block_diag(jnp.ones((4, HEAD_DIM, HEAD_DIM), BF16))
    ck = _ffn_chunks(d_ff)
    n_ck = d_ff // ck

    tm_lat, tm_ctx = TOKEN_TILE, n_ctx
    h, hc = x, ctx
    for l in range(depth):
        need_ctx = l < depth - 1
        mod = mod_all[l]
        g1 = norm1_g[l].reshape(1, d)
        g2 = norm2_g[l].reshape(1, d)
        w_in_l = w_in[l].astype(BF16)
        qg = jnp.tile(ga_q_norm_g[l], 4).reshape(1, 256)
        kg = jnp.tile(ga_k_norm_g[l], 2).reshape(1, 128)
        sink = wa_sink[l]
        conv_w = lru_conv_w[l]
        conv_b = lru_conv_b[l].reshape(1, d_lru)
        w_out_l = w_out[l].astype(BF16)
        wu = w_up[l].astype(BF16)
        wu = jnp.concatenate([wu[:, :d_ff].reshape(d, n_ck, ck), wu[:, d_ff:].reshape(d, n_ck, ck)], axis=-1)
        wu = wu.transpose(1, 0, 2)
        fcw = ffn_conv_w[l]
        fcw = jnp.concatenate([fcw[:, :d_ff].reshape(-1, n_ck, ck), fcw[:, d_ff:].reshape(-1, n_ck, ck)], axis=-1)
        fcw = fcw.transpose(1, 0, 2)
        fcb = ffn_conv_b[l]
        fcb = jnp.concatenate([fcb[:d_ff].reshape(n_ck, 1, ck), fcb[d_ff:].reshape(n_ck, 1, ck)], axis=-1)
        wd = w_down[l].astype(BF16).reshape(n_ck, ck, d)

        wg = [(0.5 * jnp.concatenate([_block_diag(lru_w_a[l, dr]), _block_diag(lru_w_x[l, dr])], axis=1)).astype(BF16)
              for dr in range(2)]
        bg = [0.5 * jnp.concatenate([lru_b_a[l, dr], lru_b_x[l, dr]]).reshape(1, 2 * d_lru) for dr in range(2)]
        lam = [lru_lam[l, dr].reshape(1, d_lru) for dr in range(2)]
        zero_state = jnp.zeros((bsz, 1, d_lru), F32)

        u_c, gate_c, hf_c, last_f, gq_c, gk_c, gv_c, wq_c, wk_c, wv_c = _in_projection(
            hc, mod, ctx_row, g1, w_in_l, seg, qg, kg, None, None,
            conv_w, conv_b, wg[0], bg[0], lam[0], zero_state, tm_ctx)
        u_l, gate_l, hf_l, _, gq_l, gk_l, gv_l, wq_l, wk_l, wv_l = _in_projection(
            h, mod, lat_row, g1, w_in_l, seg, qg, kg, cos, sin,
            conv_w, conv_b, wg[0], bg[0], lam[0], last_f, tm_lat)

        yb_l = _global_attention(gq_l, gk_l, gv_l, gk_c, gv_c, ATTN_Q_TILE, ATTN_K_CHUNK)
        yc_l = _window_attention(sink, wq_l, wk_l, wv_l, wk_c, wv_c, WINDOW_BLOCKS)

        if need_ctx:
            yb_c, yc_c = _context_attention(sink, gq_c, gk_c, gv_c, wq_c, wk_c, wv_c)
            hc1, ac2, last_b = _lru_backward(u_c, wg[1], bg[1], lam[1], zero_state, n_ctx,
                                             proj=(hf_c, gate_c, hc, yb_c, yc_c, w_out_l, mod, ctx_row, g2))
        else:
            (last_b,) = _lru_backward(u_c, wg[1], bg[1], lam[1], zero_state, n_ctx)
        h1, a2, _ = _lru_backward(u_l, wg[1], bg[1], lam[1], last_b, tm_lat,
                                  proj=(hf_l, gate_l, h, yb_l, yc_l, w_out_l, mod, lat_row, g2))

        final_g = None if need_ctx else final_norm_g.reshape(1, d)
        h = _conv_ffn(a2, h1, wu, fcw, fcb, wd, mod, lat_row, final_g, tm_lat)
        if need_ctx:
            hc = _conv_ffn(ac2, hc1, wu, fcw, fcb, wd, mod, ctx_row, None, tm_ctx)
    return h
```
